```python
import math
import jax, jax.numpy as jnp
from jax import lax
import numpy as np

D_MODEL = 2048
BATCH = 4
SEQ = 2048
DEPTH = 4

GRID_W = 64
CTX_LEN = 256
N_AB_LAYERS = (DEPTH + 1) // 2
N_C_LAYERS = DEPTH // 2
NORM_EPS = 1e-6

ML_WIDTH = D_MODEL // 2
ML_HEAD_DIM = 128
ML_HEADS = ML_WIDTH // ML_HEAD_DIM
ML_CHUNK = 64
N_GATES = 4 * ML_HEADS

RG_WIDTH = D_MODEL // 2
RG_BLOCKS = 8
RG_BLOCK_DIM = RG_WIDTH // RG_BLOCKS
RG_CONV = 4
RG_C = 8.0

AB_IN = 4 * ML_WIDTH + N_GATES + 2 * RG_WIDTH
AB_SPLITS = (ML_WIDTH, 2 * ML_WIDTH, 3 * ML_WIDTH, 4 * ML_WIDTH,
             4 * ML_WIDTH + N_GATES, 4 * ML_WIDTH + N_GATES + RG_WIDTH)
AB_OUT = ML_WIDTH + RG_WIDTH

AT_HEAD_DIM = 128
AT_HEADS = D_MODEL // AT_HEAD_DIM
AT_KV_HEADS = 4
AT_GROUP = AT_HEADS // AT_KV_HEADS
AT_Q_DIM = AT_HEADS * AT_HEAD_DIM
AT_KV_DIM = AT_KV_HEADS * AT_HEAD_DIM
QBLOCK = 128
ROPE_AXIS_DIM = AT_HEAD_DIM // 2
ROPE_THETA = 10000.0

N_EXPERTS = 16
EC_FACTOR = 2
EXPERT_FF = 1536

kernel_name = "hybrid_mlstm_rglru_gqa_ecmoe_dit"


def rmsnorm(x, g):
    xf = x.astype(jnp.float32)
    y = xf * lax.rsqrt(jnp.mean(xf * xf, axis=-1, keepdims=True) + NORM_EPS)
    return (y * g.astype(jnp.float32)).astype(x.dtype)


def modulate(h, shift, scale):
    return h * (1 + scale) + shift


def axial_rope_tables(T):
    rows = T // GRID_W
    row_ids = jnp.repeat(jnp.arange(rows), GRID_W).astype(jnp.float32)
    col_ids = jnp.tile(jnp.arange(GRID_W), rows).astype(jnp.float32)
    inv = ROPE_THETA ** (-jnp.arange(0, ROPE_AXIS_DIM, 2, dtype=jnp.float32) / ROPE_AXIS_DIM)
    ang_r = row_ids[:, None] * inv
    ang_c = col_ids[:, None] * inv
    ang = jnp.concatenate([ang_r, ang_r, ang_c, ang_c], axis=-1)
    return jnp.cos(ang), jnp.sin(ang)


def apply_rope(x, cos, sin):
    xf = x.astype(jnp.float32)
    xr = xf.reshape(*x.shape[:-1], 2, 2, ROPE_AXIS_DIM // 2)
    rot = jnp.stack([-xr[..., 1, :], xr[..., 0, :]], axis=-2).reshape(x.shape)
    return (xf * cos[:, None, :] + rot * sin[:, None, :]).astype(x.dtype)


def mlstm_chunkwise(q, k, v, li, lf, state):
    B, T, H, Dh = q.shape
    L = ML_CHUNK
    nc = T // L

    def chunks(a):
        return a.astype(jnp.float32).reshape(B, nc, L, H, -1).transpose(1, 0, 3, 2, 4)

    qc, kc, vc = chunks(q), chunks(k), chunks(v)
    lic = chunks(li[..., None])[..., 0]
    lfc = chunks(lf[..., None])[..., 0]
    tri = jnp.tril(jnp.ones((L, L), dtype=bool))

    def step(carry, inp):
        C, n, m = carry
        qt, kt, vt, it, ft = inp
        b = jnp.cumsum(ft, axis=-1)
        d = jnp.where(tri, b[..., :, None] - b[..., None, :] + it[..., None, :], -jnp.inf)
        inter = b + m[..., None]
        m_t = jnp.maximum(inter, jnp.max(d, axis=-1))
        w_inter = jnp.exp(inter - m_t)
        s = jnp.einsum('bhtd,bhsd->bhts', qt, kt) * jnp.exp(d - m_t[..., None])
        num = (w_inter[..., None] * jnp.einsum('bhvd,bhtd->bhtv', C, qt)
               + jnp.einsum('bhts,bhsv->bhtv', s, vt))
        den = w_inter * jnp.einsum('bhd,bhtd->bht', n, qt) + jnp.sum(s, axis=-1)
        h = num / jnp.maximum(jnp.abs(den), jnp.exp(-m_t))[..., None]
        b_end = b[..., -1]
        g = b_end[..., None] - b + it
        m_new = jnp.maximum(b_end + m, jnp.max(g, axis=-1))
        w_old = jnp.exp(b_end + m - m_new)
        w_tok = jnp.exp(g - m_new[..., None])
        C_new = w_old[..., None, None] * C + jnp.einsum('bhs,bhsv,bhsd->bhvd', w_tok, vt, kt)
        n_new = w_old[..., None] * n + jnp.einsum('bhs,bhsd->bhd', w_tok, kt)
        return (C_new, n_new, m_new), h

    state, h = lax.scan(step, state, (qc, kc, vc, lic, lfc))
    return state, h.transpose(1, 0, 3, 2, 4).reshape(B, T, H, Dh)


def _linrec(e1, e2):
    a1, b1 = e1
    a2, b2 = e2
    return a1 * a2, a2 * b1 + b2


def rglru_scan(x, h0, cw, cb, wa, ba, wx, bx, lam):
    B, T, R = x.shape
    xp = jnp.pad(x, ((0, 0), (RG_CONV - 1, 0), (0, 0)))
    xc = cb + sum(xp[:, j:j + T] * cw[j] for j in range(RG_CONV))
    xb = xc.reshape(B, T, RG_BLOCKS, RG_BLOCK_DIM)
    r = jax.nn.sigmoid(jnp.einsum('btnc,ncd->btnd', xb, wa).reshape(B, T, R).astype(jnp.float32) + ba)
    i = jax.nn.sigmoid(jnp.einsum('btnc,ncd->btnd', xb, wx).reshape(B, T, R).astype(jnp.float32) + bx)
    log_a = -RG_C * r * jax.nn.softplus(-lam.astype(jnp.float32))
    a = jnp.exp(log_a)
    u = jnp.sqrt(-jnp.expm1(2 * log_a)) * (i * xc.astype(jnp.float32))
    a_cum, h = lax.associative_scan(_linrec, (a, u), axis=1)
    h = h + a_cum * h0[:, None, :]
    return h, h[:, -1]


def mixer_ab(h_x, h_c, w_in, gate_b, ml_norm, conv_w, conv_b, wa, ba, wx, bx, lam, w_out, need_ctx):
    B = h_x.shape[0]

    def project(h):
        T = h.shape[1]
        q, k, v, o, g, rx, ry = jnp.split(h @ w_in, AB_SPLITS, axis=-1)
        heads = lambda a: a.reshape(B, T, ML_HEADS, ML_HEAD_DIM)
        g = (g.astype(jnp.float32) + gate_b.reshape(-1)).reshape(B, T, 2, 2, ML_HEADS)
        return heads(q), heads(k) * ML_HEAD_DIM ** -0.5, heads(v), o, g, rx, ry

    xs, cs = project(h_x), project(h_c)

    def ml_inputs(s, d, rev):
        q, k, v, _, g, _, _ = s
        arrs = (q, k, v, g[:, :, d, 0], jax.nn.log_sigmoid(g[:, :, d, 1]))
        return tuple(jnp.flip(a, 1) for a in arrs) if rev else arrs

    flip = lambda a, rev: jnp.flip(a, 1) if rev else a
    ml_x = 0.0
    ml_c = 0.0
    rg_x = 0.0
    rg_c = 0.0
    for d in range(2):
        rev = d == 1
        st0 = (jnp.zeros((B, ML_HEADS, ML_HEAD_DIM, ML_HEAD_DIM), jnp.float32),
               jnp.zeros((B, ML_HEADS, ML_HEAD_DIM), jnp.float32),
               jnp.zeros((B, ML_HEADS), jnp.float32))
        st_c, hc = mlstm_chunkwise(*ml_inputs(cs, d, rev), st0)
        _, hx = mlstm_chunkwise(*ml_inputs(xs, d, rev), st_c)
        ml_c = ml_c + flip(hc, rev)
        ml_x = ml_x + flip(hx, rev)
        prm = (conv_w[d], conv_b[d], wa[d], ba[d], wx[d], bx[d], lam[d])
        h0 = jnp.zeros((B, RG_WIDTH), jnp.float32)
        rc, rc_last = rglru_scan(flip(cs[5], rev), h0, *prm)
        rx_, _ = rglru_scan(flip(xs[5], rev), rc_last, *prm)
        rg_c = rg_c + flip(rc, rev)
        rg_x = rg_x + flip(rx_, rev)

    def merge(ml, rg, s):
        o, ry = s[3], s[6]
        T = ml.shape[1]
        y_ml = rmsnorm(ml, ml_norm.reshape(ML_HEADS, ML_HEAD_DIM)).reshape(B, T, ML_WIDTH)
        y_ml = (y_ml * jax.nn.sigmoid(o.astype(jnp.float32))).astype(o.dtype)
        y_rg = (rg * jax.nn.gelu(ry.astype(jnp.float32))).astype(ry.dtype)
        return jnp.concatenate([y_ml, y_rg], axis=-1) @ w_out

    y_x = merge(ml_x, rg_x, xs)
    y_c = merge(ml_c, rg_c, cs) if need_ctx else None
    return y_x, y_c


def gqa_attend(q, k, v):
    s = jnp.einsum('bqkgd,bskd->bkgqs', q, k, preferred_element_type=jnp.float32) * AT_HEAD_DIM ** -0.5
    p = jax.nn.softmax(s, axis=-1).astype(v.dtype)
    return jnp.einsum('bkgqs,bskd->bqkgd', p, v)


def mixer_attn(h_x, h_c, w_qkv, q_norm, k_norm, w_o, cos, sin, need_ctx):
    B, T, _ = h_x.shape
    Nc = h_c.shape[1]
    heads = lambda a, n: a.reshape(a.shape[0], a.shape[1], n, AT_HEAD_DIM)
    q_x, k_x, v_x = jnp.split(h_x @ w_qkv, (AT_Q_DIM, AT_Q_DIM + AT_KV_DIM), axis=-1)
    q_x = apply_rope(rmsnorm(heads(q_x, AT_HEADS), q_norm), cos, sin)
    k_x = apply_rope(rmsnorm(heads(k_x, AT_KV_HEADS), k_norm), cos, sin)
    v_x = heads(v_x, AT_KV_HEADS)
    k_c, v_c = jnp.split(h_c @ w_qkv[:, AT_Q_DIM:], (AT_KV_DIM,), axis=-1)
    k_c = rmsnorm(heads(k_c, AT_KV_HEADS), k_norm)
    v_c = heads(v_c, AT_KV_HEADS)
    keys = jnp.concatenate([k_c, k_x], axis=1)
    vals = jnp.concatenate([v_c, v_x], axis=1)
    qb = q_x.reshape(B, T // QBLOCK, QBLOCK, AT_KV_HEADS, AT_GROUP, AT_HEAD_DIM).transpose(1, 0, 2, 3, 4, 5)
    o_x = lax.map(lambda qq: gqa_attend(qq, keys, vals), qb)
    o_x = o_x.transpose(1, 0, 2, 3, 4, 5).reshape(B, T, AT_Q_DIM) @ w_o
    if need_ctx:
        q_c = rmsnorm(heads(h_c @ w_qkv[:, :AT_Q_DIM], AT_HEADS), q_norm)
        q_c = q_c.reshape(B, Nc, AT_KV_HEADS, AT_GROUP, AT_HEAD_DIM)
        o_c = gqa_attend(q_c, k_c, v_c).reshape(B, Nc, AT_Q_DIM) @ w_o
    else:
        o_c = None
    return o_x, o_c


def expert_choice_moe(h, w_router, w_gate, w_up, w_down):
    B, N, _ = h.shape
    cap = EC_FACTOR * N // N_EXPERTS
    aff = jax.nn.softmax(jnp.einsum('bnd,de->bne', h, w_router).astype(jnp.float32), axis=-1)
    gates, idx = lax.top_k(aff.transpose(0, 2, 1), cap)
    bidx = jnp.arange(B)[:, None, None]
    xe = h[bidx, idx]
    g = jnp.einsum('becd,edf->becf', xe, w_gate)
    u = jnp.einsum('becd,edf->becf', xe, w_up)
    y = jnp.einsum('becf,efd->becd', jax.nn.silu(g) * u, w_down) * gates[..., None].astype(h.dtype)
    return jnp.zeros_like(h).at[bidx, idx].add(y)


def setup_inputs(seed: int = 0) -> dict:
    key = jax.random.key(seed)
    ks = jax.random.split(key, 32)
    f32 = jnp.float32
    nrm = lambda k, shape, scale: jax.random.normal(k, shape, f32) * scale
    D = D_MODEL
    f_base = jnp.linspace(3.0, 6.0, ML_HEADS, dtype=f32)
    z = jnp.zeros((ML_HEADS,), f32)
    gate_base = jnp.stack([z, f_base, z, f_base])
    u = jax.random.uniform(ks[13], (N_AB_LAYERS, 2, RG_WIDTH), f32, minval=0.9, maxval=0.999)
    s = u ** (1.0 / RG_C)
    return {
        "x": nrm(ks[0], (BATCH, SEQ, D), 1.0),
        "c": nrm(ks[1], (BATCH, D), 1.0),
        "ctx": nrm(ks[2], (BATCH, CTX_LEN, D), 1.0),
        "c_ctx": nrm(ks[3], (D,), 1.0),
        "w_mod": nrm(ks[4], (DEPTH, D, 6 * D), 0.5 * D ** -0.5),
        "b_mod": nrm(ks[5], (DEPTH, 6 * D), 0.02),
        "norm_mix": 1.0 + nrm(ks[6], (DEPTH, D), 0.02),
        "norm_ffn": 1.0 + nrm(ks[7], (DEPTH, D), 0.02),
        "ab_w_in": nrm(ks[8], (N_AB_LAYERS, D, AB_IN), D ** -0.5),
        "ab_gate_b": gate_base + nrm(ks[9], (N_AB_LAYERS, 4, ML_HEADS), 0.1),
        "ml_norm": 1.0 + nrm(ks[10], (N_AB_LAYERS, ML_WIDTH), 0.02),
        "rg_conv_w": nrm(ks[11], (N_AB_LAYERS, 2, RG_CONV, RG_WIDTH), RG_CONV ** -0.5),
        "rg_conv_b": nrm(ks[12], (N_AB_LAYERS, 2, RG_WIDTH), 0.02),
        "rg_wa": nrm(ks[14], (N_AB_LAYERS, 2, RG_BLOCKS, RG_BLOCK_DIM, RG_BLOCK_DIM), RG_BLOCK_DIM ** -0.5),
        "rg_ba": nrm(ks[15], (N_AB_LAYERS, 2, RG_WIDTH), 0.02),
        "rg_wx": nrm(ks[16], (N_AB_LAYERS, 2, RG_BLOCKS, RG_BLOCK_DIM, RG_BLOCK_DIM), RG_BLOCK_DIM ** -0.5),
        "rg_bx": nrm(ks[17], (N_AB_LAYERS, 2, RG_WIDTH), 0.02),
        "rg_lam": jnp.log(s) - jnp.log1p(-s),
        "ab_w_out": nrm(ks[18], (N_AB_LAYERS, AB_OUT, D), AB_OUT ** -0.5),
        "at_w_qkv": nrm(ks[19], (N_C_LAYERS, D, AT_Q_DIM + 2 * AT_KV_DIM), D ** -0.5),
        "at_q_norm": 1.0 + nrm(ks[20], (N_C_LAYERS, AT_HEAD_DIM), 0.02),
        "at_k_norm": 1.0 + nrm(ks[21], (N_C_LAYERS, AT_HEAD_DIM), 0.02),
        "at_w_o": nrm(ks[22], (N_C_LAYERS, AT_Q_DIM, D), AT_Q_DIM ** -0.5),
        "moe_w_router": nrm(ks[23], (DEPTH, D, N_EXPERTS), D ** -0.5),
        "moe_w_gate": nrm(ks[24], (DEPTH, N_EXPERTS, D, EXPERT_FF), D ** -0.5),
        "moe_w_up": nrm(ks[25], (DEPTH, N_EXPERTS, D, EXPERT_FF), D ** -0.5),
        "moe_w_down": nrm(ks[26], (DEPTH, N_EXPERTS, EXPERT_FF, D), EXPERT_FF ** -0.5),
    }


def reference(x, c, ctx, c_ctx, w_mod, b_mod, norm_mix, norm_ffn, ab_w_in, ab_gate_b, ml_norm,
              rg_conv_w, rg_conv_b, rg_wa, rg_ba, rg_wx, rg_bx, rg_lam, ab_w_out,
              at_w_qkv, at_q_norm, at_k_norm, at_w_o,
              moe_w_router, moe_w_gate, moe_w_up, moe_w_down):
    B, T, D = x.shape
    cos, sin = axial_rope_tables(T)
    sc = jax.nn.silu(c)
    scc = jax.nn.silu(c_ctx)
    for l in range(DEPTH):
        need_ctx = l < DEPTH - 1
        j = l // 2
        mod_x = (sc @ w_mod[l] + b_mod[l]).reshape(B, 1, 6, D)
        mod_c = (scc @ w_mod[l] + b_mod[l]).reshape(1, 1, 6, D)
        hx = modulate(rmsnorm(x, norm_mix[l]), mod_x[:, :, 0], mod_x[:, :, 1])
        hc = modulate(rmsnorm(ctx, norm_mix[l]), mod_c[:, :, 0], mod_c[:, :, 1])
        if l % 2 == 0:
            yx, yc = mixer_ab(hx, hc, ab_w_in[j], ab_gate_b[j], ml_norm[j], rg_conv_w[j], rg_conv_b[j],
                              rg_wa[j], rg_ba[j], rg_wx[j], rg_bx[j], rg_lam[j], ab_w_out[j], need_ctx)
        else:
            yx, yc = mixer_attn(hx, hc, at_w_qkv[j], at_q_norm[j], at_k_norm[j], at_w_o[j],
                                cos, sin, need_ctx)
        x = x + mod_x[:, :, 2] * yx
        hx = modulate(rmsnorm(x, norm_ffn[l]), mod_x[:, :, 3], mod_x[:, :, 4])
        x = x + mod_x[:, :, 5] * expert_choice_moe(hx, moe_w_router[l], moe_w_gate[l], moe_w_up[l], moe_w_down[l])
        if need_ctx:
            ctx = ctx + mod_c[:, :, 2] * yc
            hc = modulate(rmsnorm(ctx, norm_ffn[l]), mod_c[:, :, 3], mod_c[:, :, 4])
            ctx = ctx + mod_c[:, :, 5] * expert_choice_moe(hc, moe_w_router[l], moe_w_gate[l], moe_w_up[l], moe_w_down[l])
    return x
```

```python
import functools
import math

import jax
import jax.numpy as jnp
from jax import lax
from jax.experimental import pallas as pl
from jax.experimental.pallas import tpu as pltpu

F32 = jnp.float32
BF16 = jnp.bfloat16
HIGHEST = lax.Precision.HIGHEST

NORM_EPS = 1e-6
GRID_W = 64
ROW_TILE = 256
LANES = 128
ML_HEADS = 8
ML_HEAD_DIM = 128
ML_WIDTH = ML_HEADS * ML_HEAD_DIM
N_GATES = 4 * ML_HEADS
RG_WIDTH = 1024
RG_BLOCKS = 8
RG_BLOCK_DIM = RG_WIDTH // RG_BLOCKS
RG_CONV = 4
RG_C = 8.0
AT_HEAD_DIM = 128
AT_HEADS = 16
AT_KV_HEADS = 4
AT_GROUP = AT_HEADS // AT_KV_HEADS
ROPE_AXIS_DIM = AT_HEAD_DIM // 2
ROPE_THETA = 10000.0
N_EXPERTS = 16
EC_FACTOR = 2
MM_ROWS = 1152
HALO = 8
VMEM_LIMIT = 56 * 1024 * 1024


def _params(sem, vmem=VMEM_LIMIT):
    return pltpu.CompilerParams(dimension_semantics=sem, vmem_limit_bytes=vmem)


def _sigmoid(x):
    return 1.0 / (1.0 + jnp.exp(-x))


def _softplus(x):
    return jnp.maximum(x, 0.0) + jnp.log1p(jnp.exp(-jnp.abs(x)))


def _log_sigmoid(x):
    return -_softplus(-x)


def _mods_kernel(c_ref, w_ref, b_ref, o_ref):
    c = c_ref[...]
    a = c * _sigmoid(c)
    o_ref[0] = jnp.dot(a, w_ref[0], preferred_element_type=F32, precision=HIGHEST) + b_ref[0]


def _mods(c8, w_mod, b_mod):
    depth, d, n = w_mod.shape
    tn = 1024
    return pl.pallas_call(
        _mods_kernel,
        out_shape=jax.ShapeDtypeStruct((depth, 8, n), F32),
        grid=(depth, n // tn),
        in_specs=[pl.BlockSpec((8, d), lambda l, j: (0, 0)),
                  pl.BlockSpec((1, d, tn), lambda l, j: (l, 0, j)),
                  pl.BlockSpec((1, 1, tn), lambda l, j: (l, 0, j))],
        out_specs=pl.BlockSpec((1, 8, tn), lambda l, j: (l, 0, j)),
        compiler_params=_params(("parallel", "parallel")),
        name="mods",
    )(c8, w_mod, b_mod.reshape(depth, 1, n))


def _mod_spec(d):
    return pl.BlockSpec((1, 1, 6, d), lambda b, i: (b, jnp.minimum(i, 1), 0, 0))


def _rms(x, g):
    return x * lax.rsqrt(jnp.mean(x * x, axis=-1, keepdims=True) + NORM_EPS) * g


def _norm_kernel(x_ref, g_ref, m_ref, o_ref):
    m = m_ref[0, 0]
    y = _rms(x_ref[0], g_ref[...])
    o_ref[0] = (y * (1.0 + m[1:2]) + m[0:1]).astype(o_ref.dtype)


def _norm_mod(xs, g, mt):
    b, s, d = xs.shape
    return pl.pallas_call(
        _norm_kernel,
        out_shape=jax.ShapeDtypeStruct((b, s, d), BF16),
        grid=(b, s // ROW_TILE),
        in_specs=[pl.BlockSpec((1, ROW_TILE, d), lambda b, i: (b, i, 0)),
                  pl.BlockSpec((1, d), lambda b, i: (0, 0)),
                  _mod_spec(d)],
        out_specs=pl.BlockSpec((1, ROW_TILE, d), lambda b, i: (b, i, 0)),
        compiler_params=_params(("parallel", "parallel")),
        name="norm_mod",
    )(xs, g.reshape(1, d), mt)


def _mm_kernel(a_ref, w_ref, o_ref):
    o_ref[...] = jnp.dot(a_ref[...], w_ref[...].astype(BF16),
                         preferred_element_type=F32).astype(o_ref.dtype)


def _matmul(a, w, n, *, tn, out_dtype=F32, name="matmul"):
    m, k = a.shape
    tm = math.gcd(m, MM_ROWS)
    return pl.pallas_call(
        _mm_kernel,
        out_shape=jax.ShapeDtypeStruct((m, n), out_dtype),
        grid=(m // tm, n // tn),
        in_specs=[pl.BlockSpec((tm, k), lambda i, j: (i, 0)),
                  pl.BlockSpec((k, tn), lambda i, j: (0, j))],
        out_specs=pl.BlockSpec((tm, tn), lambda i, j: (i, j)),
        compiler_params=_params(("parallel", "parallel")),
        name=name,
    )(a, w)


def _scan_chunk(c, n_chunks, rev):
    return jnp.where(jnp.logical_or(c == 0, rev == 0), c, n_chunks - c)


def _mlstm_chunk(rev, q_ref, k_ref, v_ref, gc_ref, gr_ref, o_ref, ct_ref, n_ref, m_ref):
    L = ROW_TILE
    head = pl.program_id(1)
    gi = int(rev) * 2 * ML_HEADS + head
    gf = gi + ML_HEADS
    gc = gc_ref[0]
    gr = gr_ref[0]
    lane = lax.broadcasted_iota(jnp.int32, gc.shape, 1)
    sub = lax.broadcasted_iota(jnp.int32, gr.shape, 0)
    i_col = jnp.sum(jnp.where(lane == gi, gc, 0.0), axis=1, keepdims=True)
    f_col = _log_sigmoid(jnp.sum(jnp.where(lane == gf, gc, 0.0), axis=1, keepdims=True))
    i_row = jnp.sum(jnp.where(sub == gi, gr, 0.0), axis=0, keepdims=True)
    f_row = _log_sigmoid(jnp.sum(jnp.where(sub == gf, gr, 0.0), axis=0, keepdims=True))

    row = lax.broadcasted_iota(jnp.int32, (L, L), 0)
    col = lax.broadcasted_iota(jnp.int32, (L, L), 1)
    vis = (col >= row) if rev else (col <= row)
    vis_t = (row >= col) if rev else (row <= col)
    b_col = jnp.sum(jnp.where(vis, f_row, 0.0), axis=1, keepdims=True)
    b_row = jnp.sum(jnp.where(vis_t, f_col, 0.0), axis=0, keepdims=True)
    d = jnp.where(vis, b_col - b_row + i_row, -jnp.inf)
    m_prev = m_ref[...]
    inter = b_col + m_prev
    m_t = jnp.maximum(inter, jnp.max(d, axis=1, keepdims=True))
    w_inter = jnp.exp(inter - m_t)

    q = q_ref[0]
    ks = k_ref[0] * (ML_HEAD_DIM ** -0.5)
    qb = q.astype(BF16)
    kb = ks.astype(BF16)
    vb = v_ref[0].astype(BF16)
    ct = ct_ref[...]
    n_row = n_ref[...]
    s = lax.dot_general(qb, kb, (((1,), (1,)), ((), ())), preferred_element_type=F32)
    s = s * jnp.exp(d - m_t)
    num = (w_inter * jnp.dot(qb, ct.astype(BF16), preferred_element_type=F32)
           + jnp.dot(s.astype(BF16), vb, preferred_element_type=F32))
    den = w_inter * jnp.sum(q * n_row, axis=1, keepdims=True) + jnp.sum(s, axis=1, keepdims=True)
    o_ref[0, 0] = num / jnp.maximum(jnp.abs(den), jnp.exp(-m_t))

    b_end = jnp.sum(f_row, axis=1, keepdims=True)
    g_col = b_end - b_col + i_col
    m_new = jnp.maximum(b_end + m_prev, jnp.max(g_col, axis=0, keepdims=True))
    w_old = jnp.exp(b_end + m_prev - m_new)
    kw = ks * jnp.exp(g_col - m_new)
    ct_ref[...] = w_old * ct + lax.dot_general(kw.astype(BF16), vb, (((0,), (0,)), ((), ())),
                                               preferred_element_type=F32)
    n_ref[...] = w_old * n_row + jnp.sum(kw, axis=0, keepdims=True)
    m_ref[...] = m_new


def _mlstm_kernel(q_ref, k_ref, v_ref, gc_ref, gr_ref, o_ref, ct_ref, n_ref, m_ref):
    @pl.when(pl.program_id(3) == 0)
    def _():
        ct_ref[...] = jnp.zeros_like(ct_ref)
        n_ref[...] = jnp.zeros_like(n_ref)
        m_ref[...] = jnp.zeros_like(m_ref)

    for rev in (False, True):
        @pl.when(pl.program_id(2) == int(rev))
        def _():
            _mlstm_chunk(rev, q_ref, k_ref, v_ref, gc_ref, gr_ref, o_ref, ct_ref, n_ref, m_ref)


def _mlstm(p1, gcol, grow):
    b, s, _ = p1.shape
    nch = s // ROW_TILE
    L = ROW_TILE

    def qkv_spec(off):
        return pl.BlockSpec((1, L, ML_HEAD_DIM),
                            lambda b, h, r, c: (b, _scan_chunk(c, nch, r), off + h))

    return pl.pallas_call(
        _mlstm_kernel,
        out_shape=jax.ShapeDtypeStruct((2, b, s, ML_WIDTH), F32),
        grid=(b, ML_HEADS, 2, nch),
        in_specs=[qkv_spec(0), qkv_spec(ML_HEADS), qkv_spec(2 * ML_HEADS),
                  pl.BlockSpec((1, L, N_GATES), lambda b, h, r, c: (b, _scan_chunk(c, nch, r), 0)),
                  pl.BlockSpec((1, N_GATES, L), lambda b, h, r, c: (b, 0, _scan_chunk(c, nch, r)))],
        out_specs=pl.BlockSpec((1, 1, L, ML_HEAD_DIM),
                               lambda b, h, r, c: (r, b, _scan_chunk(c, nch, r), h)),
        scratch_shapes=[pltpu.VMEM((ML_HEAD_DIM, ML_HEAD_DIM), F32),
                        pltpu.VMEM((1, ML_HEAD_DIM), F32),
                        pltpu.VMEM((1, 1), F32)],
        compiler_params=_params(("parallel", "parallel", "parallel", "arbitrary")),
        name="mlstm",
    )(p1, p1, p1, gcol, grow)


def _rglru_chunk(rev, x_ref, cw_ref, cb_ref, wa_ref, ba_ref, wx_ref, bx_ref, lam_ref, o_ref,
                 ext_ref, a_ref, u_ref, h_ref):
    L = ROW_TILE
    c = pl.program_id(2)
    halo = slice(HALO + L, 2 * HALO + L) if rev else slice(0, HALO)

    @pl.when(c <= 1)
    def _():
        ext_ref[halo, :] = jnp.zeros((HALO, RG_WIDTH), F32)

    @pl.when(c == 0)
    def _():
        h_ref[...] = jnp.zeros_like(h_ref)

    x = x_ref[0]
    ext_ref[HALO:HALO + L, :] = x
    xc = cb_ref[0]
    for j in range(RG_CONV):
        k = RG_CONV - 1 - j
        lo = HALO + k if rev else HALO - k
        xc = xc + cw_ref[0, j:j + 1, :] * ext_ref[lo:lo + L, :]
    ext_ref[halo, :] = x[0:HALO] if rev else x[L - HALO:L]

    sp = _softplus(-lam_ref[0])
    xcb = xc.astype(BF16)
    for n in range(RG_BLOCKS):
        blk = slice(n * RG_BLOCK_DIM, (n + 1) * RG_BLOCK_DIM)
        xb = xcb[:, blk]
        r = _sigmoid(jnp.dot(xb, wa_ref[0, n].astype(BF16), preferred_element_type=F32)
                     + ba_ref[0, :, blk])
        i = _sigmoid(jnp.dot(xb, wx_ref[0, n].astype(BF16), preferred_element_type=F32)
                     + bx_ref[0, :, blk])
        log_a = -RG_C * r * sp[:, blk]
        a_ref[:, blk] = jnp.exp(log_a)
        th = jnp.tanh(log_a)
        u_ref[:, blk] = jnp.sqrt(-2.0 * th / (1.0 - th)) * (i * xc[:, blk])

    def step(t, h):
        tt = (L - 1 - t) if rev else t
        h = a_ref[pl.ds(tt, 1), :] * h + u_ref[pl.ds(tt, 1), :]
        o_ref[0, 0, pl.ds(tt, 1), :] = h
        return h

    h_ref[...] = lax.fori_loop(0, L, step, h_ref[...], unroll=8)


def _rglru_kernel(*refs):
    for rev in (False, True):
        @pl.when(pl.program_id(1) == int(rev))
        def _():
            _rglru_chunk(rev, *refs)


def _rglru(p2, cw, cb, wa, ba, wx, bx, lam):
    b, s, _ = p2.shape
    nch = s // ROW_TILE
    L = ROW_TILE
    R = RG_WIDTH
    vec = lambda a: a.reshape(2, 1, R)
    vspec = pl.BlockSpec((1, 1, R), lambda b, r, c: (r, 0, 0))
    wspec = pl.BlockSpec((1, RG_BLOCKS, RG_BLOCK_DIM, RG_BLOCK_DIM), lambda b, r, c: (r, 0, 0, 0))
    return pl.pallas_call(
        _rglru_kernel,
        out_shape=jax.ShapeDtypeStruct((2, b, s, R), F32),
        grid=(b, 2, nch),
        in_specs=[pl.BlockSpec((1, L, R), lambda b, r, c: (b, _scan_chunk(c, nch, r), 0)),
                  pl.BlockSpec((1, RG_CONV, R), lambda b, r, c: (r, 0, 0)),
                  vspec, wspec, vspec, wspec, vspec, vspec],
        out_specs=pl.BlockSpec((1, 1, L, R), lambda b, r, c: (r, b, _scan_chunk(c, nch, r), 0)),
        scratch_shapes=[pltpu.VMEM((L + 2 * HALO, R), F32),
                        pltpu.VMEM((L, R), F32),
                        pltpu.VMEM((L, R), F32),
                        pltpu.VMEM((1, R), F32)],
        compiler_params=_params(("parallel", "parallel", "arbitrary")),
        name="rglru",
    )(p2, cw, vec(cb), wa, vec(ba), wx, vec(bx), vec(lam))


def _gelu_tanh(x):
    return x * (0.5 * (1.0 + jnp.tanh(math.sqrt(2.0 / math.pi) * (x + 0.044715 * (x * x * x)))))


def _merge_kernel(mf_ref, mr_ref, rf_ref, rr_ref, o_ref, ry_ref, g_ref, y_ref):
    ml = mf_ref[0, 0] + mr_ref[0, 0]
    o = o_ref[0]
    g = g_ref[...]
    for h in range(ML_HEADS):
        blk = slice(h * ML_HEAD_DIM, (h + 1) * ML_HEAD_DIM)
        y = _rms(ml[:, blk], g[:, blk]) * _sigmoid(o[:, blk])
        y_ref[0, :, blk] = y.astype(y_ref.dtype)
    rg = rf_ref[0, 0] + rr_ref[0, 0]
    y_ref[0, :, ML_WIDTH:] = (rg * _gelu_tanh(ry_ref[0])).astype(y_ref.dtype)


def _merge(hml, hrg, p1, p2, ml_norm):
    _, b, s, _ = hml.shape
    L = ROW_TILE
    W = ML_WIDTH
    dspec = lambda r: pl.BlockSpec((1, 1, L, W), lambda b, i: (r, b, i, 0))
    return pl.pallas_call(
        _merge_kernel,
        out_shape=jax.ShapeDtypeStruct((b, s, 2 * W), BF16),
        grid=(b, s // L),
        in_specs=[dspec(0), dspec(1), dspec(0), dspec(1),
                  pl.BlockSpec((1, L, W), lambda b, i: (b, i, 3)),
                  pl.BlockSpec((1, L, W), lambda b, i: (b, i, 1)),
                  pl.BlockSpec((1, W), lambda b, i: (0, 0))],
        out_specs=pl.BlockSpec((1, L, 2 * W), lambda b, i: (b, i, 0)),
        compiler_params=_params(("parallel", "parallel")),
        name="merge",
    )(hml, hml, hrg, hrg, p1, p2, ml_norm.reshape(1, W))


def _rope_tables(s_ctx, t):
    rows = t // GRID_W
    row_ids = jnp.repeat(jnp.arange(rows), GRID_W).astype(F32)
    col_ids = jnp.tile(jnp.arange(GRID_W), rows).astype(F32)
    inv = ROPE_THETA ** (-jnp.arange(0, ROPE_AXIS_DIM, 2, dtype=F32) / ROPE_AXIS_DIM)
    ang_r = row_ids[:, None] * inv
    ang_c = col_ids[:, None] * inv
    ang = jnp.concatenate([ang_r, ang_r, ang_c, ang_c], axis=-1)
    cos = jnp.concatenate([jnp.ones((s_ctx, AT_HEAD_DIM), F32), jnp.cos(ang)], axis=0)
    sin = jnp.concatenate([jnp.zeros((s_ctx, AT_HEAD_DIM), F32), jnp.sin(ang)], axis=0)
    return cos, sin


def _rope(x, cos, sin):
    half = ROPE_AXIS_DIM // 2
    lane = lax.broadcasted_iota(jnp.int32, x.shape, 1)
    first = (lane % ROPE_AXIS_DIM) < half
    rot = jnp.where(first, -pltpu.roll(x, AT_HEAD_DIM - half, 1), pltpu.roll(x, half, 1))
    return x * cos + rot * sin


def _qkprep_kernel(p_ref, qn_ref, kn_ref, cos_ref, sin_ref, q_ref, k_ref, v_ref):
    cos = cos_ref[...]
    sin = sin_ref[...]
    Dh = AT_HEAD_DIM
    for h in range(AT_HEADS):
        x = _rope(_rms(p_ref[0, :, h * Dh:(h + 1) * Dh], qn_ref[...]), cos, sin)
        q_ref[0, :, h * Dh:(h + 1) * Dh] = (x * (Dh ** -0.5)).astype(q_ref.dtype)
    for h in range(AT_KV_HEADS):
        off = (AT_HEADS + h) * Dh
        x = _rope(_rms(p_ref[0, :, off:off + Dh], kn_ref[...]), cos, sin)
        k_ref[0, :, h * Dh:(h + 1) * Dh] = x.astype(k_ref.dtype)
    off = (AT_HEADS + AT_KV_HEADS) * Dh
    v_ref[0] = p_ref[0, :, off:off + AT_KV_HEADS * Dh].astype(v_ref.dtype)


def _qkprep(p, q_norm, k_norm, cos, sin):
    b, s, w = p.shape
    L = ROW_TILE
    Dh = AT_HEAD_DIM
    qd, kd = AT_HEADS * Dh, AT_KV_HEADS * Dh
    return pl.pallas_call(
        _qkprep_kernel,
        out_shape=(jax.ShapeDtypeStruct((b, s, qd), BF16),
                   jax.ShapeDtypeStruct((b, s, kd), BF16),
                   jax.ShapeDtypeStruct((b, s, kd), BF16)),
        grid=(b, s // L),
        in_specs=[pl.BlockSpec((1, L, w), lambda b, i: (b, i, 0)),
                  pl.BlockSpec((1, Dh), lambda b, i: (0, 0)),
                  pl.BlockSpec((1, Dh), lambda b, i: (0, 0)),
                  pl.BlockSpec((L, Dh), lambda b, i: (i, 0)),
                  pl.BlockSpec((L, Dh), lambda b, i: (i, 0))],
        out_specs=(pl.BlockSpec((1, L, qd), lambda b, i: (b, i, 0)),
                   pl.BlockSpec((1, L, kd), lambda b, i: (b, i, 0)),
                   pl.BlockSpec((1, L, kd), lambda b, i: (b, i, 0))),
        compiler_params=_params(("parallel", "parallel")),
        name="qkprep",
    )(p, q_norm.reshape(1, Dh), k_norm.reshape(1, Dh), cos, sin)


def _attend(q_ref, k, v, o_ref):
    Dh = AT_HEAD_DIM
    for g in range(AT_GROUP):
        q = q_ref[0, :, g * Dh:(g + 1) * Dh]
        s = lax.dot_general(q, k, (((1,), (1,)), ((), ())), preferred_element_type=F32)
        e = jnp.exp(s - jnp.max(s, axis=-1, keepdims=True))
        o = jnp.dot(e.astype(BF16), v, preferred_element_type=F32)
        o_ref[0, :, g * Dh:(g + 1) * Dh] = (o / jnp.sum(e, axis=-1, keepdims=True)).astype(o_ref.dtype)


def _attn_kernel(q_ref, k_ref, v_ref, o_ref):
    i = pl.program_id(2)

    @pl.when(i == 0)
    def _():
        _attend(q_ref, k_ref[0, 0:ROW_TILE, :], v_ref[0, 0:ROW_TILE, :], o_ref)

    @pl.when(i > 0)
    def _():
        _attend(q_ref, k_ref[0], v_ref[0], o_ref)


def _attention(q, k, v):
    b, s, qd = q.shape
    L = ROW_TILE
    Dh = AT_HEAD_DIM
    gw = AT_GROUP * Dh
    return pl.pallas_call(
        _attn_kernel,
        out_shape=jax.ShapeDtypeStruct((b, s, qd), BF16),
        grid=(b, AT_KV_HEADS, s // L),
        in_specs=[pl.BlockSpec((1, L, gw), lambda b, h, i: (b, i, h)),
                  pl.BlockSpec((1, s, Dh), lambda b, h, i: (b, 0, h)),
                  pl.BlockSpec((1, s, Dh), lambda b, h, i: (b, 0, h))],
        out_specs=pl.BlockSpec((1, L, gw), lambda b, h, i: (b, i, h)),
        compiler_params=_params(("parallel", "parallel", "parallel")),
        name="attention",
    )(q, k, v)


def _resnorm_kernel(x_ref, y_ref, g_ref, m_ref, wr_ref, xo_ref, h_ref, lg_ref):
    m = m_ref[0, 0]
    x = x_ref[0] + m[2:3] * y_ref[0]
    xo_ref[0] = x
    h = _rms(x, g_ref[...]) * (1.0 + m[4:5]) + m[3:4]
    h_ref[0] = h.astype(h_ref.dtype)
    lg_ref[0] = jnp.dot(h, wr_ref[...], preferred_element_type=F32, precision=HIGHEST)


def _resnorm(xs, y, g, mt, wr):
    b, s, d = xs.shape
    L = ROW_TILE
    tile = pl.BlockSpec((1, L, d), lambda b, i: (b, i, 0))
    return pl.pallas_call(
        _resnorm_kernel,
        out_shape=(jax.ShapeDtypeStruct((b, s, d), F32),
                   jax.ShapeDtypeStruct((b, s, d), BF16),
                   jax.ShapeDtypeStruct((b, s, LANES), F32)),
        grid=(b, s // L),
        in_specs=[tile, tile, pl.BlockSpec((1, d), lambda b, i: (0, 0)), _mod_spec(d),
                  pl.BlockSpec((d, LANES), lambda b, i: (0, 0))],
        out_specs=(tile, tile, pl.BlockSpec((1, L, LANES), lambda b, i: (b, i, 0))),
        compiler_params=_params(("parallel", "parallel")),
        name="resnorm",
    )(xs, y, g.reshape(1, d), mt, wr)


def _excl_prefix(mask):
    e, n = mask.shape
    groups = n // LANES
    r = lax.broadcasted_iota(jnp.int32, (LANES, LANES), 0)
    c = lax.broadcasted_iota(jnp.int32, (LANES, LANES), 1)
    upper = jnp.where(r < c, 1.0, 0.0).astype(BF16)
    stk = jnp.concatenate([mask[:, g * LANES:(g + 1) * LANES] for g in range(groups)], axis=0)
    within = jnp.dot(stk.astype(BF16), upper, preferred_element_type=F32)
    tot = jnp.sum(stk, axis=1, keepdims=True)
    outs = []
    off = jnp.zeros((e, 1), F32)
    for g in range(groups):
        outs.append(within[g * e:(g + 1) * e] + off)
        off = off + tot[g * e:(g + 1) * e]
    return jnp.concatenate(outs, axis=1)


def _top_slots(aff, cap):
    v = lax.bitcast_convert_type(aff, jnp.int32)
    thr = jnp.zeros((aff.shape[0], 1), jnp.int32)
    for bit in range(30, -1, -1):
        cand = thr | (1 << bit)
        cnt = jnp.sum(jnp.where(v >= cand, 1.0, 0.0), axis=1, keepdims=True)
        thr = jnp.where(cnt >= cap, cand, thr)
    gt = jnp.where(v > thr, 1.0, 0.0)
    eq = jnp.where(v == thr, 1.0, 0.0)
    need = cap - jnp.sum(gt, axis=1, keepdims=True)
    sel = gt + eq * jnp.where(_excl_prefix(eq) < need, 1.0, 0.0)
    return jnp.where(sel > 0.0, _excl_prefix(sel), -1.0)


def _route_kernel(lg_ref, pos_ref, aff_ref, post_ref):
    s = lg_ref.shape[1]
    E = N_EXPERTS
    lt = lg_ref[0].T[0:E, :]
    ex = jnp.exp(lt - jnp.max(lt, axis=0, keepdims=True))
    aff = ex / jnp.sum(ex, axis=0, keepdims=True)
    aff_ref[0] = aff
    pos = jnp.concatenate(
        [_top_slots(aff[:, :ROW_TILE], EC_FACTOR * ROW_TILE // E),
         _top_slots(aff[:, ROW_TILE:], EC_FACTOR * (s - ROW_TILE) // E)], axis=1)
    pos_ref[0] = pos
    padded = jnp.concatenate([pos, jnp.full((LANES - E, s), -1.0, F32)], axis=0)
    post_ref[0] = padded.T


def _route(logits):
    b, s, _ = logits.shape
    E = N_EXPERTS
    return pl.pallas_call(
        _route_kernel,
        out_shape=(jax.ShapeDtypeStruct((b, E, s), F32),
                   jax.ShapeDtypeStruct((b, E, s), F32),
                   jax.ShapeDtypeStruct((b, s, LANES), F32)),
        grid=(b,),
        in_specs=[pl.BlockSpec((1, s, LANES), lambda b: (b, 0, 0))],
        out_specs=(pl.BlockSpec((1, E, s), lambda b: (b, 0, 0)),
                   pl.BlockSpec((1, E, s), lambda b: (b, 0, 0)),
                   pl.BlockSpec((1, s, LANES), lambda b: (b, 0, 0))),
        compiler_params=_params(("parallel",)),
        name="route",
    )(logits)


def _gather_kernel(h_ref, pos_ref, aff_ref, xx_ref, xc_ref, gx_ref, gc_ref):
    s = h_ref.shape[1]
    for lo, n, x_ref, g_ref in ((0, ROW_TILE, xc_ref, gc_ref), (ROW_TILE, s - ROW_TILE, xx_ref, gx_ref)):
        cap = x_ref.shape[1]
        p = pos_ref[0, 0, :, lo:lo + n]
        a = aff_ref[0, 0, :, lo:lo + n]
        slot = lax.broadcasted_iota(jnp.int32, (cap, n), 0).astype(F32)
        hit = p == slot
        onehot = jnp.where(hit, 1.0, 0.0).astype(BF16)
        x_ref[0] = jnp.dot(onehot, h_ref[0, lo:lo + n, :], preferred_element_type=F32).astype(x_ref.dtype)
        g_ref[0] = jnp.sum(jnp.where(hit, a, 0.0), axis=1, keepdims=True)


def _gather(h, pos, aff):
    b, s, d = h.shape
    E = N_EXPERTS
    cap_c = EC_FACTOR * ROW_TILE // E
    cap_x = EC_FACTOR * (s - ROW_TILE) // E
    row = pl.BlockSpec((1, 1, 1, s), lambda b, e: (b, e, 0, 0))
    return pl.pallas_call(
        _gather_kernel,
        out_shape=(jax.ShapeDtypeStruct((E, b * cap_x, d), BF16),
                   jax.ShapeDtypeStruct((E, b * cap_c, d), BF16),
                   jax.ShapeDtypeStruct((E, b * cap_x, 1), F32),
                   jax.ShapeDtypeStruct((E, b * cap_c, 1), F32)),
        grid=(b, E),
        in_specs=[pl.BlockSpec((1, s, d), lambda b, e: (b, 0, 0)), row, row],
        out_specs=(pl.BlockSpec((1, cap_x, d), lambda b, e: (e, b, 0)),
                   pl.BlockSpec((1, cap_c, d), lambda b, e: (e, b, 0)),
                   pl.BlockSpec((1, cap_x, 1), lambda b, e: (e, b, 0)),
                   pl.BlockSpec((1, cap_c, 1), lambda b, e: (e, b, 0))),
        compiler_params=_params(("parallel", "arbitrary")),
        name="gather",
    )(h, pos.reshape(b, E, 1, s), aff.reshape(b, E, 1, s))


def _ffn_kernel(xx_ref, xc_ref, gx_ref, gc_ref, wg_ref, wu_ref, wd_ref, yx_ref, yc_ref, ax_ref, ac_ref):
    f = pl.program_id(1)
    wg = wg_ref[0].astype(BF16)
    wu = wu_ref[0].astype(BF16)
    wd = wd_ref[0].astype(BF16)
    for x_ref, acc_ref, g_ref, y_ref in ((xx_ref, ax_ref, gx_ref, yx_ref), (xc_ref, ac_ref, gc_ref, yc_ref)):
        x = x_ref[0]
        g = jnp.dot(x, wg, preferred_element_type=F32)
        u = jnp.dot(x, wu, preferred_element_type=F32)
        a = (g * _sigmoid(g) * u).astype(BF16)
        part = jnp.dot(a, wd, preferred_element_type=F32)

        @pl.when(f == 0)
        def _():
            acc_ref[...] = part

        @pl.when(f > 0)
        def _():
            acc_ref[...] += part

        @pl.when(f == pl.num_programs(1) - 1)
        def _():
            y_ref[0] = (acc_ref[...] * g_ref[0]).astype(y_ref.dtype)


def _ffn(xx, xc, gx, gc, w_gate, w_up, w_down):
    E, rx, d = xx.shape
    rc = xc.shape[1]
    ff = w_gate.shape[2]
    tf = 256
    xspec = lambda r, w: pl.BlockSpec((1, r, w), lambda e, f: (e, 0, 0))
    return pl.pallas_call(
        _ffn_kernel,
        out_shape=(jax.ShapeDtypeStruct((E, rx, d), BF16), jax.ShapeDtypeStruct((E, rc, d), BF16)),
        grid=(E, ff // tf),
        in_specs=[xspec(rx, d), xspec(rc, d), xspec(rx, 1), xspec(rc, 1),
                  pl.BlockSpec((1, d, tf), lambda e, f: (e, 0, f)),
                  pl.BlockSpec((1, d, tf), lambda e, f: (e, 0, f)),
                  pl.BlockSpec((1, tf, d), lambda e, f: (e, f, 0))],
        out_specs=(xspec(rx, d), xspec(rc, d)),
        scratch_shapes=[pltpu.VMEM((rx, d), F32), pltpu.VMEM((rc, d), F32)],
        compiler_params=_params(("parallel", "arbitrary")),
        name="expert_ffn",
    )(xx, xc, gx, gc, w_gate, w_up, w_down)


def _combine(pt, y_ref):
    cap = y_ref.shape[1]
    n = pt.shape[0]
    slot = lax.broadcasted_iota(jnp.int32, (n, cap), 1).astype(F32)
    acc = jnp.zeros((n, y_ref.shape[2]), F32)
    for e in range(N_EXPERTS):
        onehot = jnp.where(pt[:, e:e + 1] == slot, 1.0, 0.0).astype(BF16)
        acc = acc + jnp.dot(onehot, y_ref[e], preferred_element_type=F32)
    return acc


def _scatter_kernel(x_ref, pt_ref, yx_ref, yc_ref, m_ref, o_ref):
    i = pl.program_id(1)
    gate = m_ref[0, 0][5:6]

    @pl.when(i == 0)
    def _():
        o_ref[0] = x_ref[0] + gate * _combine(pt_ref[0], yc_ref)

    @pl.when(i > 0)
    def _():
        o_ref[0] = x_ref[0] + gate * _combine(pt_ref[0], yx_ref)


def _scatter(xs, post, yx, yc, mt):
    b, s, d = xs.shape
    E = N_EXPERTS
    L = ROW_TILE
    cap_x = yx.shape[1] // b
    cap_c = yc.shape[1] // b
    tile = pl.BlockSpec((1, L, d), lambda b, i: (b, i, 0))
    return pl.pallas_call(
        _scatter_kernel,
        out_shape=jax.ShapeDtypeStruct((b, s, d), F32),
        grid=(b, s // L),
        in_specs=[tile,
                  pl.BlockSpec((1, L, LANES), lambda b, i: (b, i, 0)),
                  pl.BlockSpec((E, cap_x, d), lambda b, i: (0, b, 0)),
                  pl.BlockSpec((E, cap_c, d), lambda b, i: (0, b, 0)),
                  _mod_spec(d)],
        out_specs=tile,
        compiler_params=_params(("parallel", "arbitrary")),
        name="scatter",
    )(xs, post, yx, yc, mt)


def kernel(x, c, ctx, c_ctx, w_mod, b_mod, norm_mix, norm_ffn, ab_w_in, ab_gate_b, ml_norm,
           rg_conv_w, rg_conv_b, rg_wa, rg_ba, rg_wx, rg_bx, rg_lam, ab_w_out,
           at_w_qkv, at_q_norm, at_k_norm, at_w_o,
           moe_w_router, moe_w_gate, moe_w_up, moe_w_down):
    B, T, D = x.shape
    n_ctx = ctx.shape[1]
    assert n_ctx == ROW_TILE and T % ROW_TILE == 0 and B <= 7
    S = n_ctx + T
    depth = w_mod.shape[0]
    E = moe_w_router.shape[2]

    c8 = jnp.concatenate([c, c_ctx[None], jnp.zeros((7 - B, D), F32)], axis=0)
    modv = _mods(c8, w_mod, b_mod).reshape(depth, 8, 6, D)
    mod_c = jnp.broadcast_to(modv[:, B][:, None], (depth, B, 6, D))
    mtab = jnp.stack([mod_c, modv[:, :B]], axis=2)
    cos, sin = _rope_tables(n_ctx, T)
    wr_pad = jnp.pad(moe_w_router, ((0, 0), (0, 0), (0, LANES - E)))

    xs = jnp.concatenate([ctx, x], axis=1)
    for l in range(depth):
        j = l // 2
        mt = mtab[l]
        h = _norm_mod(xs, norm_mix[l], mt).reshape(B * S, D)
        if l % 2 == 0:
            w_in = ab_w_in[j]
            g0 = 4 * ML_WIDTH
            p1 = _matmul(h, w_in, g0, tn=512, name="ab_in_qkvo").reshape(B, S, g0)
            w_rg = w_in[:, g0 + N_GATES:]
            p2 = _matmul(h, w_rg, 2 * RG_WIDTH, tn=512, name="ab_in_rg").reshape(B, S, 2 * RG_WIDTH)
            w_g = jnp.pad(w_in[:, g0:g0 + N_GATES], ((0, 0), (0, LANES - N_GATES)))
            gates = _matmul(h, w_g, LANES, tn=LANES, name="ab_in_gates").reshape(B, S, LANES)
            gcol = gates[:, :, :N_GATES] + ab_gate_b[j].reshape(-1)
            hml = _mlstm(p1, gcol, jnp.swapaxes(gcol, 1, 2))
            hrg = _rglru(p2, rg_conv_w[j], rg_conv_b[j], rg_wa[j], rg_ba[j], rg_wx[j], rg_bx[j], rg_lam[j])
            mixed = _merge(hml, hrg, p1, p2, ml_norm[j]).reshape(B * S, D)
            y = _matmul(mixed, ab_w_out[j], D, tn=512, name="ab_out")
        else:
            w_qkv = at_w_qkv[j]
            p = _matmul(h, w_qkv, w_qkv.shape[1], tn=512, name="at_qkv").reshape(B, S, w_qkv.shape[1])
            q, k, v = _qkprep(p, at_q_norm[j], at_k_norm[j], cos, sin)
            o = _attention(q, k, v).reshape(B * S, D)
            y = _matmul(o, at_w_o[j], D, tn=512, name="at_out")
        xs, h2, logits = _resnorm(xs, y.reshape(B, S, D), norm_ffn[l], mt, wr_pad[l])
        pos, aff, post = _route(logits)
        xx, xc, gx, gc = _gather(h2, pos, aff)
        yx, yc = _ffn(xx, xc, gx, gc, moe_w_gate[l], moe_w_up[l], moe_w_down[l])
        xs = _scatter(xs, post, yx, yc, mt)
    return xs[:, n_ctx:]
```

```python
import functools
import math

import jax
import jax.numpy as jnp
from jax import lax
from jax.experimental import pallas as pl
from jax.experimental.pallas import tpu as pltpu

F32 = jnp.float32
BF16 = jnp.bfloat16
HIGHEST = lax.Precision.HIGHEST

NORM_EPS = 1e-6
GRID_W = 64
ROW_TILE = 256
LANES = 128
ML_HEADS = 8
ML_HEAD_DIM = 128
ML_WIDTH = ML_HEADS * ML_HEAD_DIM
N_GATES = 4 * ML_HEADS
RG_WIDTH = 1024
RG_BLOCKS = 8
RG_BLOCK_DIM = RG_WIDTH // RG_BLOCKS
RG_CONV = 4
RG_C = 8.0
AT_HEAD_DIM = 128
AT_HEADS = 16
AT_KV_HEADS = 4
AT_GROUP = AT_HEADS // AT_KV_HEADS
ROPE_AXIS_DIM = AT_HEAD_DIM // 2
ROPE_THETA = 10000.0
N_EXPERTS = 16
EC_FACTOR = 2
MM_ROWS = 1152
FFN_HIDDEN_TILE = 256
FFN_OUT_TILE = 512
HALO = 8
VMEM_LIMIT = 56 * 1024 * 1024


def _params(sem, vmem=VMEM_LIMIT):
    return pltpu.CompilerParams(dimension_semantics=sem, vmem_limit_bytes=vmem)


def _sigmoid(x):
    return 1.0 / (1.0 + jnp.exp(-x))


def _softplus(x):
    return jnp.maximum(x, 0.0) + jnp.log1p(jnp.exp(-jnp.abs(x)))


def _log_sigmoid(x):
    return -_softplus(-x)


def _mods_kernel(c_ref, w_ref, b_ref, o_ref):
    c = c_ref[...]
    a = c * _sigmoid(c)
    o_ref[0] = jnp.dot(a.astype(BF16), w_ref[0].astype(BF16), preferred_element_type=F32) + b_ref[0]


def _mods(c8, w_mod, b_mod):
    depth, d, n = w_mod.shape
    tn = 1024
    return pl.pallas_call(
        _mods_kernel,
        out_shape=jax.ShapeDtypeStruct((depth, 8, n), F32),
        grid=(depth, n // tn),
        in_specs=[pl.BlockSpec((8, d), lambda l, j: (0, 0)),
                  pl.BlockSpec((1, d, tn), lambda l, j: (l, 0, j)),
                  pl.BlockSpec((1, 1, tn), lambda l, j: (l, 0, j))],
        out_specs=pl.BlockSpec((1, 8, tn), lambda l, j: (l, 0, j)),
        compiler_params=_params(("parallel", "parallel")),
        name="mods",
    )(c8, w_mod, b_mod.reshape(depth, 1, n))


def _mod_spec(d):
    return pl.BlockSpec((1, 1, 6, d), lambda b, i: (b, jnp.minimum(i, 1), 0, 0))


def _rms(x, g):
    return x * lax.rsqrt(jnp.mean(x * x, axis=-1, keepdims=True) + NORM_EPS) * g


def _norm_kernel(x_ref, g_ref, m_ref, o_ref):
    m = m_ref[0, 0]
    y = _rms(x_ref[0], g_ref[...])
    o_ref[0] = (y * (1.0 + m[1:2]) + m[0:1]).astype(o_ref.dtype)


def _norm_mod(xs, g, mt):
    b, s, d = xs.shape
    return pl.pallas_call(
        _norm_kernel,
        out_shape=jax.ShapeDtypeStruct((b, s, d), BF16),
        grid=(b, s // ROW_TILE),
        in_specs=[pl.BlockSpec((1, ROW_TILE, d), lambda b, i: (b, i, 0)),
                  pl.BlockSpec((1, d), lambda b, i: (0, 0)),
                  _mod_spec(d)],
        out_specs=pl.BlockSpec((1, ROW_TILE, d), lambda b, i: (b, i, 0)),
        compiler_params=_params(("parallel", "parallel")),
        name="norm_mod",
    )(xs, g.reshape(1, d), mt)


def _mm_kernel(a_ref, w_ref, o_ref):
    o_ref[...] = jnp.dot(a_ref[...], w_ref[...].astype(BF16),
                         preferred_element_type=F32).astype(o_ref.dtype)


def _matmul(a, w, n, *, tn, layer=None, out_dtype=F32, name="matmul"):
    m, k = a.shape
    tm = math.gcd(m, MM_ROWS)
    if layer is None:
        wspec = pl.BlockSpec((k, tn), lambda i, j: (0, j))
    else:
        wspec = pl.BlockSpec((None, k, tn), lambda i, j: (layer, 0, j))
    return pl.pallas_call(
        _mm_kernel,
        out_shape=jax.ShapeDtypeStruct((m, n), out_dtype),
        grid=(m // tm, n // tn),
        in_specs=[pl.BlockSpec((tm, k), lambda i, j: (i, 0)), wspec],
        out_specs=pl.BlockSpec((tm, tn), lambda i, j: (i, j)),
        compiler_params=_params(("parallel", "parallel")),
        name=name,
    )(a, w)


def _scan_chunk(c, n_chunks, rev):
    return jnp.where(jnp.logical_or(c == 0, rev == 0), c, n_chunks - c)


def _mlstm_head(rev, head, vis, gc, gr, lfr, bc, br, q_ref, k_ref, v_ref, o_ref, c_ref, n_ref, m_ref):
    Dh = ML_HEAD_DIM
    blk = slice(head * Dh, (head + 1) * Dh)
    gi = int(rev) * 2 * ML_HEADS + head
    gf = gi + ML_HEADS
    nt = (((1,), (1,)), ((), ()))
    i_row = gr[gi:gi + 1, :]
    f_row = lfr[gf:gf + 1, :]
    b_row = br[gf:gf + 1, :]
    src = gc[:, gi:gi + 1] - bc[:, gf:gf + 1]
    d = jnp.where(vis, b_row + src, -jnp.inf)
    m_prev = m_ref[head:head + 1, :]
    inter = b_row + m_prev
    m_t = jnp.maximum(inter, jnp.max(d, axis=0, keepdims=True))
    w_inter = jnp.exp(inter - m_t)

    qb = q_ref[0, :, blk].astype(BF16)
    kb = (k_ref[0, :, blk] * (Dh ** -0.5)).astype(BF16)
    vt = v_ref[0, :, blk].T
    c = c_ref[head]
    n_row = n_ref[head:head + 1, :]
    n8 = jnp.broadcast_to(n_row, (8, Dh)).astype(BF16)
    s = lax.dot_general(kb, qb, nt, preferred_element_type=F32) * jnp.exp(d - m_t)
    num = (w_inter * lax.dot_general(c.astype(BF16), qb, nt, preferred_element_type=F32)
           + jnp.dot(vt.astype(BF16), s.astype(BF16), preferred_element_type=F32))
    den = (w_inter * lax.dot_general(n8, qb, nt, preferred_element_type=F32)[0:1]
           + jnp.sum(s, axis=0, keepdims=True))
    o_ref[0, 0, :, blk] = (num / jnp.maximum(jnp.abs(den), jnp.exp(-m_t))).T

    b_end = jnp.sum(f_row, axis=1, keepdims=True)
    g_row = b_end - b_row + i_row
    m_new = jnp.maximum(b_end + m_prev, jnp.max(g_row, axis=1, keepdims=True))
    w_old = jnp.exp(b_end + m_prev - m_new)
    w_tok = jnp.exp(g_row - m_new)
    c_ref[head] = w_old * c + jnp.dot((vt * w_tok).astype(BF16), kb, preferred_element_type=F32)
    w8 = jnp.broadcast_to(w_tok, (8, w_tok.shape[1])).astype(BF16)
    n_ref[head:head + 1, :] = w_old * n_row + jnp.dot(w8, kb, preferred_element_type=F32)[0:1]
    m_ref[head:head + 1, :] = m_new


def _mlstm_kernel(q_ref, k_ref, v_ref, gc_ref, gr_ref, o_ref, c_ref, n_ref, m_ref):
    L = ROW_TILE

    @pl.when(pl.program_id(2) == 0)
    def _():
        c_ref[...] = jnp.zeros_like(c_ref)
        n_ref[...] = jnp.zeros_like(n_ref)
        m_ref[...] = jnp.zeros_like(m_ref)

    row = lax.broadcasted_iota(jnp.int32, (L, L), 0)
    col = lax.broadcasted_iota(jnp.int32, (L, L), 1)
    gc = gc_ref[0]
    gr = gr_ref[0]
    lfc = _log_sigmoid(gc)
    lfr = _log_sigmoid(gr)
    for rev in (False, True):
        @pl.when(pl.program_id(1) == int(rev))
        def _():
            vis_qs = (col >= row) if rev else (col <= row)
            vis_sq = (row >= col) if rev else (row <= col)
            bc = jnp.dot(jnp.where(vis_qs, 1.0, 0.0), lfc, preferred_element_type=F32, precision=HIGHEST)
            br = jnp.dot(lfr, jnp.where(vis_sq, 1.0, 0.0), preferred_element_type=F32, precision=HIGHEST)
            for head in range(ML_HEADS):
                _mlstm_head(rev, head, vis_sq, gc, gr, lfr, bc, br, q_ref, k_ref, v_ref, o_ref,
                            c_ref, n_ref, m_ref)


def _mlstm(p1, gcol, grow):
    b, s, _ = p1.shape
    nch = s // ROW_TILE
    L = ROW_TILE
    W = ML_WIDTH

    def qkv_spec(off):
        return pl.BlockSpec((1, L, W), lambda b, r, c: (b, _scan_chunk(c, nch, r), off))

    return pl.pallas_call(
        _mlstm_kernel,
        out_shape=jax.ShapeDtypeStruct((2, b, s, W), F32),
        grid=(b, 2, nch),
        in_specs=[qkv_spec(0), qkv_spec(1), qkv_spec(2),
                  pl.BlockSpec((1, L, N_GATES), lambda b, r, c: (b, _scan_chunk(c, nch, r), 0)),
                  pl.BlockSpec((1, N_GATES, L), lambda b, r, c: (b, 0, _scan_chunk(c, nch, r)))],
        out_specs=pl.BlockSpec((1, 1, L, W), lambda b, r, c: (r, b, _scan_chunk(c, nch, r), 0)),
        scratch_shapes=[pltpu.VMEM((ML_HEADS, ML_HEAD_DIM, ML_HEAD_DIM), F32),
                        pltpu.VMEM((ML_HEADS, ML_HEAD_DIM), F32),
                        pltpu.VMEM((ML_HEADS, 1), F32)],
        compiler_params=_params(("parallel", "parallel", "arbitrary")),
        name="mlstm",
    )(p1, p1, p1, gcol, grow)


def _rglru_chunk(rev, x_ref, cw_ref, cb_ref, wa_ref, ba_ref, wx_ref, bx_ref, lam_ref, o_ref,
                 ext_ref, a_ref, u_ref, h_ref):
    L = ROW_TILE
    c = pl.program_id(2)
    halo = slice(HALO + L, 2 * HALO + L) if rev else slice(0, HALO)

    @pl.when(c <= 1)
    def _():
        ext_ref[halo, :] = jnp.zeros((HALO, RG_WIDTH), F32)

    @pl.when(c == 0)
    def _():
        h_ref[...] = jnp.zeros_like(h_ref)

    x = x_ref[0]
    ext_ref[HALO:HALO + L, :] = x
    xc = cb_ref[0]
    for j in range(RG_CONV):
        k = RG_CONV - 1 - j
        lo = HALO + k if rev else HALO - k
        xc = xc + cw_ref[0, j:j + 1, :] * ext_ref[lo:lo + L, :]
    ext_ref[halo, :] = x[0:HALO] if rev else x[L - HALO:L]

    sp = _softplus(-lam_ref[0])
    xcb = xc.astype(BF16)
    for n in range(RG_BLOCKS):
        blk = slice(n * RG_BLOCK_DIM, (n + 1) * RG_BLOCK_DIM)
        xb = xcb[:, blk]
        r = _sigmoid(jnp.dot(xb, wa_ref[0, n].astype(BF16), preferred_element_type=F32)
                     + ba_ref[0, :, blk])
        i = _sigmoid(jnp.dot(xb, wx_ref[0, n].astype(BF16), preferred_element_type=F32)
                     + bx_ref[0, :, blk])
        log_a = -RG_C * r * sp[:, blk]
        a_ref[:, blk] = jnp.exp(log_a)
        th = jnp.tanh(log_a)
        u_ref[:, blk] = jnp.sqrt(-2.0 * th / (1.0 - th)) * (i * xc[:, blk])

    def step(t, h):
        tt = (L - 1 - t) if rev else t
        h = a_ref[pl.ds(tt, 1), :] * h + u_ref[pl.ds(tt, 1), :]
        o_ref[0, 0, pl.ds(tt, 1), :] = h
        return h

    h_ref[...] = lax.fori_loop(0, L, step, h_ref[...], unroll=8)


def _rglru_kernel(*refs):
    for rev in (False, True):
        @pl.when(pl.program_id(1) == int(rev))
        def _():
            _rglru_chunk(rev, *refs)


def _rglru(p2, cw, cb, wa, ba, wx, bx, lam):
    b, s, _ = p2.shape
    nch = s // ROW_TILE
    L = ROW_TILE
    R = RG_WIDTH
    vec = lambda a: a.reshape(2, 1, R)
    vspec = pl.BlockSpec((1, 1, R), lambda b, r, c: (r, 0, 0))
    wspec = pl.BlockSpec((1, RG_BLOCKS, RG_BLOCK_DIM, RG_BLOCK_DIM), lambda b, r, c: (r, 0, 0, 0))
    return pl.pallas_call(
        _rglru_kernel,
        out_shape=jax.ShapeDtypeStruct((2, b, s, R), F32),
        grid=(b, 2, nch),
        in_specs=[pl.BlockSpec((1, L, R), lambda b, r, c: (b, _scan_chunk(c, nch, r), 0)),
                  pl.BlockSpec((1, RG_CONV, R), lambda b, r, c: (r, 0, 0)),
                  vspec, wspec, vspec, wspec, vspec, vspec],
        out_specs=pl.BlockSpec((1, 1, L, R), lambda b, r, c: (r, b, _scan_chunk(c, nch, r), 0)),
        scratch_shapes=[pltpu.VMEM((L + 2 * HALO, R), F32),
                        pltpu.VMEM((L, R), F32),
                        pltpu.VMEM((L, R), F32),
                        pltpu.VMEM((1, R), F32)],
        compiler_params=_params(("parallel", "parallel", "arbitrary")),
        name="rglru",
    )(p2, cw, vec(cb), wa, vec(ba), wx, vec(bx), vec(lam))


def _gelu_tanh(x):
    return x * (0.5 * (1.0 + jnp.tanh(math.sqrt(2.0 / math.pi) * (x + 0.044715 * (x * x * x)))))


def _merge_kernel(mf_ref, mr_ref, rf_ref, rr_ref, o_ref, ry_ref, g_ref, y_ref):
    ml = mf_ref[0, 0] + mr_ref[0, 0]
    o = o_ref[0]
    g = g_ref[...]
    for h in range(ML_HEADS):
        blk = slice(h * ML_HEAD_DIM, (h + 1) * ML_HEAD_DIM)
        y = _rms(ml[:, blk], g[:, blk]) * _sigmoid(o[:, blk])
        y_ref[0, :, blk] = y.astype(y_ref.dtype)
    rg = rf_ref[0, 0] + rr_ref[0, 0]
    y_ref[0, :, ML_WIDTH:] = (rg * _gelu_tanh(ry_ref[0])).astype(y_ref.dtype)


def _merge(hml, hrg, p1, p2, ml_norm):
    _, b, s, _ = hml.shape
    L = ROW_TILE
    W = ML_WIDTH
    dspec = lambda r: pl.BlockSpec((1, 1, L, W), lambda b, i: (r, b, i, 0))
    return pl.pallas_call(
        _merge_kernel,
        out_shape=jax.ShapeDtypeStruct((b, s, 2 * W), BF16),
        grid=(b, s // L),
        in_specs=[dspec(0), dspec(1), dspec(0), dspec(1),
                  pl.BlockSpec((1, L, W), lambda b, i: (b, i, 3)),
                  pl.BlockSpec((1, L, W), lambda b, i: (b, i, 1)),
                  pl.BlockSpec((1, W), lambda b, i: (0, 0))],
        out_specs=pl.BlockSpec((1, L, 2 * W), lambda b, i: (b, i, 0)),
        compiler_params=_params(("parallel", "parallel")),
        name="merge",
    )(hml, hml, hrg, hrg, p1, p2, ml_norm.reshape(1, W))


def _rope_tables(s_ctx, t):
    rows = t // GRID_W
    row_ids = jnp.repeat(jnp.arange(rows), GRID_W).astype(F32)
    col_ids = jnp.tile(jnp.arange(GRID_W), rows).astype(F32)
    inv = ROPE_THETA ** (-jnp.arange(0, ROPE_AXIS_DIM, 2, dtype=F32) / ROPE_AXIS_DIM)
    ang_r = row_ids[:, None] * inv
    ang_c = col_ids[:, None] * inv
    ang = jnp.concatenate([ang_r, ang_r, ang_c, ang_c], axis=-1)
    cos = jnp.concatenate([jnp.ones((s_ctx, AT_HEAD_DIM), F32), jnp.cos(ang)], axis=0)
    sin = jnp.concatenate([jnp.zeros((s_ctx, AT_HEAD_DIM), F32), jnp.sin(ang)], axis=0)
    return cos, sin


def _rope(x, cos, sin):
    half = ROPE_AXIS_DIM // 2
    lane = lax.broadcasted_iota(jnp.int32, x.shape, 1)
    first = (lane % ROPE_AXIS_DIM) < half
    rot = jnp.where(first, -pltpu.roll(x, AT_HEAD_DIM - half, 1), pltpu.roll(x, half, 1))
    return x * cos + rot * sin


def _qkprep_kernel(p_ref, qn_ref, kn_ref, cos_ref, sin_ref, q_ref, k_ref, v_ref):
    cos = cos_ref[...]
    sin = sin_ref[...]
    Dh = AT_HEAD_DIM
    for h in range(AT_HEADS):
        x = _rope(_rms(p_ref[0, :, h * Dh:(h + 1) * Dh], qn_ref[...]), cos, sin)
        q_ref[0, :, h * Dh:(h + 1) * Dh] = (x * (Dh ** -0.5)).astype(q_ref.dtype)
    for h in range(AT_KV_HEADS):
        off = (AT_HEADS + h) * Dh
        x = _rope(_rms(p_ref[0, :, off:off + Dh], kn_ref[...]), cos, sin)
        k_ref[0, :, h * Dh:(h + 1) * Dh] = x.astype(k_ref.dtype)
    for h in range(AT_KV_HEADS):
        off = (AT_HEADS + AT_KV_HEADS + h) * Dh
        v_ref[0, :, 2 * h * Dh:(2 * h + 1) * Dh] = p_ref[0, :, off:off + Dh].astype(v_ref.dtype)
        v_ref[0, :, (2 * h + 1) * Dh:(2 * h + 2) * Dh] = jnp.ones((p_ref.shape[1], Dh), v_ref.dtype)


def _qkprep(p, q_norm, k_norm, cos, sin):
    b, s, w = p.shape
    L = ROW_TILE
    Dh = AT_HEAD_DIM
    qd, kd = AT_HEADS * Dh, AT_KV_HEADS * Dh
    return pl.pallas_call(
        _qkprep_kernel,
        out_shape=(jax.ShapeDtypeStruct((b, s, qd), BF16),
                   jax.ShapeDtypeStruct((b, s, kd), BF16),
                   jax.ShapeDtypeStruct((b, s, 2 * kd), BF16)),
        grid=(b, s // L),
        in_specs=[pl.BlockSpec((1, L, w), lambda b, i: (b, i, 0)),
                  pl.BlockSpec((1, Dh), lambda b, i: (0, 0)),
                  pl.BlockSpec((1, Dh), lambda b, i: (0, 0)),
                  pl.BlockSpec((L, Dh), lambda b, i: (i, 0)),
                  pl.BlockSpec((L, Dh), lambda b, i: (i, 0))],
        out_specs=(pl.BlockSpec((1, L, qd), lambda b, i: (b, i, 0)),
                   pl.BlockSpec((1, L, kd), lambda b, i: (b, i, 0)),
                   pl.BlockSpec((1, L, 2 * kd), lambda b, i: (b, i, 0))),
        compiler_params=_params(("parallel", "parallel")),
        name="qkprep",
    )(p, q_norm.reshape(1, Dh), k_norm.reshape(1, Dh), cos, sin)


def _attend(q_ref, k, v, o_ref):
    Dh = AT_HEAD_DIM
    for g in range(AT_GROUP):
        q = q_ref[0, :, g * Dh:(g + 1) * Dh]
        s = lax.dot_general(q, k, (((1,), (1,)), ((), ())), preferred_element_type=F32)
        e = jnp.exp(s - jnp.max(s, axis=-1, keepdims=True)).astype(BF16)
        o = jnp.dot(e, v, preferred_element_type=F32)
        o_ref[0, :, g * Dh:(g + 1) * Dh] = (o[:, :Dh] / o[:, Dh:]).astype(o_ref.dtype)


def _attn_kernel(q_ref, k_ref, v_ref, o_ref):
    i = pl.program_id(2)

    @pl.when(i == 0)
    def _():
        _attend(q_ref, k_ref[0, 0:ROW_TILE, :], v_ref[0, 0:ROW_TILE, :], o_ref)

    @pl.when(i > 0)
    def _():
        _attend(q_ref, k_ref[0], v_ref[0], o_ref)


def _attention(q, k, v):
    b, s, qd = q.shape
    L = ROW_TILE
    Dh = AT_HEAD_DIM
    gw = AT_GROUP * Dh
    return pl.pallas_call(
        _attn_kernel,
        out_shape=jax.ShapeDtypeStruct((b, s, qd), BF16),
        grid=(b, AT_KV_HEADS, s // L),
        in_specs=[pl.BlockSpec((1, L, gw), lambda b, h, i: (b, i, h)),
                  pl.BlockSpec((1, s, Dh), lambda b, h, i: (b, 0, h)),
                  pl.BlockSpec((1, s, 2 * Dh), lambda b, h, i: (b, 0, h))],
        out_specs=pl.BlockSpec((1, L, gw), lambda b, h, i: (b, i, h)),
        compiler_params=_params(("parallel", "parallel", "parallel")),
        name="attention",
    )(q, k, v)


def _resnorm_kernel(x_ref, y_ref, g_ref, m_ref, wr_ref, xo_ref, h_ref, lg_ref):
    m = m_ref[0, 0]
    x = x_ref[0] + m[2:3] * y_ref[0]
    xo_ref[0] = x
    h = _rms(x, g_ref[...]) * (1.0 + m[4:5]) + m[3:4]
    h_ref[0] = h.astype(h_ref.dtype)
    lg_ref[0] = jnp.dot(h, wr_ref[...], preferred_element_type=F32, precision=HIGHEST)


def _resnorm(xs, y, g, mt, wr):
    b, s, d = xs.shape
    L = ROW_TILE
    tile = pl.BlockSpec((1, L, d), lambda b, i: (b, i, 0))
    return pl.pallas_call(
        _resnorm_kernel,
        out_shape=(jax.ShapeDtypeStruct((b, s, d), F32),
                   jax.ShapeDtypeStruct((b, s, d), BF16),
                   jax.ShapeDtypeStruct((b, s, LANES), F32)),
        grid=(b, s // L),
        in_specs=[tile, tile, pl.BlockSpec((1, d), lambda b, i: (0, 0)), _mod_spec(d),
                  pl.BlockSpec((d, LANES), lambda b, i: (0, 0))],
        out_specs=(tile, tile, pl.BlockSpec((1, L, LANES), lambda b, i: (b, i, 0))),
        compiler_params=_params(("parallel", "parallel")),
        name="resnorm",
    )(xs, y, g.reshape(1, d), mt, wr)


def _excl_prefix(mask):
    e, n = mask.shape
    groups = n // LANES
    r = lax.broadcasted_iota(jnp.int32, (LANES, LANES), 0)
    c = lax.broadcasted_iota(jnp.int32, (LANES, LANES), 1)
    upper = jnp.where(r < c, 1.0, 0.0).astype(BF16)
    stk = jnp.concatenate([mask[:, g * LANES:(g + 1) * LANES] for g in range(groups)], axis=0)
    within = jnp.dot(stk.astype(BF16), upper, preferred_element_type=F32)
    tot = jnp.sum(stk, axis=1, keepdims=True)
    outs = []
    off = jnp.zeros((e, 1), F32)
    for g in range(groups):
        outs.append(within[g * e:(g + 1) * e] + off)
        off = off + tot[g * e:(g + 1) * e]
    return jnp.concatenate(outs, axis=1)


def _top_slots(aff, cap):
    v = lax.bitcast_convert_type(aff, jnp.int32)
    thr = jnp.zeros((aff.shape[0], 1), jnp.int32)
    for bit in range(30, -1, -1):
        cand = thr | (1 << bit)
        cnt = jnp.sum(jnp.where(v >= cand, 1.0, 0.0), axis=1, keepdims=True)
        thr = jnp.where(cnt >= cap, cand, thr)
    gt = jnp.where(v > thr, 1.0, 0.0)
    eq = jnp.where(v == thr, 1.0, 0.0)
    need = cap - jnp.sum(gt, axis=1, keepdims=True)
    sel = gt + eq * jnp.where(_excl_prefix(eq) < need, 1.0, 0.0)
    return jnp.where(sel > 0.0, _excl_prefix(sel), -1.0)


def _route_kernel(lg_ref, pos_ref, aff_ref, post_ref):
    s = lg_ref.shape[1]
    E = N_EXPERTS
    lt = lg_ref[0].T[0:E, :]
    ex = jnp.exp(lt - jnp.max(lt, axis=0, keepdims=True))
    aff = ex / jnp.sum(ex, axis=0, keepdims=True)
    aff_ref[0] = aff
    pos = jnp.concatenate(
        [_top_slots(aff[:, :ROW_TILE], EC_FACTOR * ROW_TILE // E),
         _top_slots(aff[:, ROW_TILE:], EC_FACTOR * (s - ROW_TILE) // E)], axis=1)
    pos_ref[0] = pos
    padded = jnp.concatenate([pos, jnp.full((LANES - E, s), -1.0, F32)], axis=0)
    post_ref[0] = padded.T


def _route(logits):
    b, s, _ = logits.shape
    E = N_EXPERTS
    return pl.pallas_call(
        _route_kernel,
        out_shape=(jax.ShapeDtypeStruct((b, E, s), F32),
                   jax.ShapeDtypeStruct((b, E, s), F32),
                   jax.ShapeDtypeStruct((b, s, LANES), F32)),
        grid=(b,),
        in_specs=[pl.BlockSpec((1, s, LANES), lambda b: (b, 0, 0))],
        out_specs=(pl.BlockSpec((1, E, s), lambda b: (b, 0, 0)),
                   pl.BlockSpec((1, E, s), lambda b: (b, 0, 0)),
                   pl.BlockSpec((1, s, LANES), lambda b: (b, 0, 0))),
        compiler_params=_params(("parallel",)),
        name="route",
    )(logits)


def _gather_kernel(h_ref, pos_ref, aff_ref, xx_ref, xc_ref, gx_ref, gc_ref):
    s = h_ref.shape[1]
    for lo, n, x_ref, g_ref in ((0, ROW_TILE, xc_ref, gc_ref), (ROW_TILE, s - ROW_TILE, xx_ref, gx_ref)):
        cap = x_ref.shape[1]
        p = pos_ref[0, 0, :, lo:lo + n]
        a = aff_ref[0, 0, :, lo:lo + n]
        slot = lax.broadcasted_iota(jnp.int32, (cap, n), 0).astype(F32)
        hit = p == slot
        onehot = jnp.where(hit, 1.0, 0.0).astype(BF16)
        x_ref[0] = jnp.dot(onehot, h_ref[0, lo:lo + n, :], preferred_element_type=F32).astype(x_ref.dtype)
        g_ref[0] = jnp.sum(jnp.where(hit, a, 0.0), axis=1, keepdims=True)


def _gather(h, pos, aff):
    b, s, d = h.shape
    E = N_EXPERTS
    cap_c = EC_FACTOR * ROW_TILE // E
    cap_x = EC_FACTOR * (s - ROW_TILE) // E
    row = pl.BlockSpec((1, 1, 1, s), lambda b, e: (b, e, 0, 0))
    return pl.pallas_call(
        _gather_kernel,
        out_shape=(jax.ShapeDtypeStruct((E, b * cap_x, d), BF16),
                   jax.ShapeDtypeStruct((E, b * cap_c, d), BF16),
                   jax.ShapeDtypeStruct((E, b * cap_x, 1), F32),
                   jax.ShapeDtypeStruct((E, b * cap_c, 1), F32)),
        grid=(b, E),
        in_specs=[pl.BlockSpec((1, s, d), lambda b, e: (b, 0, 0)), row, row],
        out_specs=(pl.BlockSpec((1, cap_x, d), lambda b, e: (e, b, 0)),
                   pl.BlockSpec((1, cap_c, d), lambda b, e: (e, b, 0)),
                   pl.BlockSpec((1, cap_x, 1), lambda b, e: (e, b, 0)),
                   pl.BlockSpec((1, cap_c, 1), lambda b, e: (e, b, 0))),
        compiler_params=_params(("parallel", "arbitrary")),
        name="gather",
    )(h, pos.reshape(b, E, 1, s), aff.reshape(b, E, 1, s))


def _ffn_kernel(xx_ref, xc_ref, gx_ref, gc_ref, wg_ref, wu_ref, wd_ref, yx_ref, yc_ref, ax_ref, ac_ref):
    step = pl.program_id(1)
    nf = ax_ref.shape[0]
    pairs = ((xx_ref, ax_ref, gx_ref, yx_ref), (xc_ref, ac_ref, gc_ref, yc_ref))

    @pl.when(step < nf)
    def _():
        wg = wg_ref[0].astype(BF16)
        wu = wu_ref[0].astype(BF16)
        for x_ref, a_ref, _, _ in pairs:
            x = x_ref[0]
            g = jnp.dot(x, wg, preferred_element_type=F32)
            u = jnp.dot(x, wu, preferred_element_type=F32)
            a_ref[step] = (g * _sigmoid(g) * u).astype(BF16)

    @pl.when(step >= nf)
    def _():
        wd = wd_ref[0].astype(BF16)
        tf = ax_ref.shape[2]
        for _, a_ref, g_ref, y_ref in pairs:
            y = jnp.dot(a_ref[0], wd[0:tf], preferred_element_type=F32)
            for f in range(1, nf):
                y = y + jnp.dot(a_ref[f], wd[f * tf:(f + 1) * tf], preferred_element_type=F32)
            y_ref[0] = (y * g_ref[0]).astype(y_ref.dtype)


def _ffn(xx, xc, gx, gc, w_gate, w_up, w_down, layer):
    E, rx, d = xx.shape
    rc = xc.shape[1]
    ff = w_gate.shape[3]
    nf = ff // FFN_HIDDEN_TILE
    nd = d // FFN_OUT_TILE
    xspec = lambda r, w: pl.BlockSpec((1, r, w), lambda e, s: (e, 0, 0))
    hid = lambda e, s: (layer, e, 0, jnp.minimum(s, nf - 1))
    out_tile = lambda s: jnp.maximum(s - nf, 0)
    yspec = lambda r: pl.BlockSpec((1, r, FFN_OUT_TILE), lambda e, s: (e, 0, out_tile(s)))
    return pl.pallas_call(
        _ffn_kernel,
        out_shape=(jax.ShapeDtypeStruct((E, rx, d), BF16), jax.ShapeDtypeStruct((E, rc, d), BF16)),
        grid=(E, nf + nd),
        in_specs=[xspec(rx, d), xspec(rc, d), xspec(rx, 1), xspec(rc, 1),
                  pl.BlockSpec((None, 1, d, FFN_HIDDEN_TILE), hid),
                  pl.BlockSpec((None, 1, d, FFN_HIDDEN_TILE), hid),
                  pl.BlockSpec((None, 1, ff, FFN_OUT_TILE), lambda e, s: (layer, e, 0, out_tile(s)))],
        out_specs=(yspec(rx), yspec(rc)),
        scratch_shapes=[pltpu.VMEM((nf, rx, FFN_HIDDEN_TILE), BF16), pltpu.VMEM((nf, rc, FFN_HIDDEN_TILE), BF16)],
        compiler_params=_params(("parallel", "arbitrary")),
        name="expert_ffn",
    )(xx, xc, gx, gc, w_gate, w_up, w_down)


def _combine(pt, y_ref):
    cap = y_ref.shape[1]
    n = pt.shape[0]
    slot = lax.broadcasted_iota(jnp.int32, (n, cap), 1).astype(F32)
    acc = jnp.zeros((n, y_ref.shape[2]), F32)
    for e in range(N_EXPERTS):
        onehot = jnp.where(pt[:, e:e + 1] == slot, 1.0, 0.0).astype(BF16)
        acc = acc + jnp.dot(onehot, y_ref[e], preferred_element_type=F32)
    return acc


def _scatter_kernel(x_ref, pt_ref, yx_ref, yc_ref, m_ref, o_ref):
    i = pl.program_id(1)
    gate = m_ref[0, 0][5:6]

    @pl.when(i == 0)
    def _():
        o_ref[0] = x_ref[0] + gate * _combine(pt_ref[0], yc_ref)

    @pl.when(i > 0)
    def _():
        o_ref[0] = x_ref[0] + gate * _combine(pt_ref[0], yx_ref)


def _scatter(xs, post, yx, yc, mt):
    b, s, d = xs.shape
    E = N_EXPERTS
    L = ROW_TILE
    cap_x = yx.shape[1] // b
    cap_c = yc.shape[1] // b
    tile = pl.BlockSpec((1, L, d), lambda b, i: (b, i, 0))
    return pl.pallas_call(
        _scatter_kernel,
        out_shape=jax.ShapeDtypeStruct((b, s, d), F32),
        grid=(b, s // L),
        in_specs=[tile,
                  pl.BlockSpec((1, L, LANES), lambda b, i: (b, i, 0)),
                  pl.BlockSpec((E, cap_x, d), lambda b, i: (0, b, 0)),
                  pl.BlockSpec((E, cap_c, d), lambda b, i: (0, b, 0)),
                  _mod_spec(d)],
        out_specs=tile,
        compiler_params=_params(("parallel", "arbitrary")),
        name="scatter",
    )(xs, post, yx, yc, mt)


def kernel(x, c, ctx, c_ctx, w_mod, b_mod, norm_mix, norm_ffn, ab_w_in, ab_gate_b, ml_norm,
           rg_conv_w, rg_conv_b, rg_wa, rg_ba, rg_wx, rg_bx, rg_lam, ab_w_out,
           at_w_qkv, at_q_norm, at_k_norm, at_w_o,
           moe_w_router, moe_w_gate, moe_w_up, moe_w_down):
    B, T, D = x.shape
    n_ctx = ctx.shape[1]
    assert n_ctx == ROW_TILE and T % ROW_TILE == 0 and B <= 7
    S = n_ctx + T
    depth = w_mod.shape[0]
    E = moe_w_router.shape[2]

    c8 = jnp.concatenate([c, c_ctx[None], jnp.zeros((7 - B, D), F32)], axis=0)
    modv = _mods(c8, w_mod, b_mod).reshape(depth, 8, 6, D)
    mod_c = jnp.broadcast_to(modv[:, B][:, None], (depth, B, 6, D))
    mtab = jnp.stack([mod_c, modv[:, :B]], axis=2)
    cos, sin = _rope_tables(n_ctx, T)
    wr_pad = jnp.pad(moe_w_router, ((0, 0), (0, 0), (0, LANES - E)))

    xs = jnp.concatenate([ctx, x], axis=1)
    for l in range(depth):
        j = l // 2
        mt = mtab[l]
        h = _norm_mod(xs, norm_mix[l], mt).reshape(B * S, D)
        if l % 2 == 0:
            g0 = 4 * ML_WIDTH
            p1 = _matmul(h, ab_w_in, g0, tn=512, layer=j, name="ab_in_qkvo").reshape(B, S, g0)
            w_rg = ab_w_in[j, :, g0 + N_GATES:]
            p2 = _matmul(h, w_rg, 2 * RG_WIDTH, tn=512, name="ab_in_rg").reshape(B, S, 2 * RG_WIDTH)
            w_g = jnp.pad(ab_w_in[j, :, g0:g0 + N_GATES], ((0, 0), (0, LANES - N_GATES)))
            gates = _matmul(h, w_g, LANES, tn=LANES, name="ab_in_gates").reshape(B, S, LANES)
            gcol = gates[:, :, :N_GATES] + ab_gate_b[j].reshape(-1)
            hml = _mlstm(p1, gcol, jnp.swapaxes(gcol, 1, 2))
            hrg = _rglru(p2, rg_conv_w[j], rg_conv_b[j], rg_wa[j], rg_ba[j], rg_wx[j], rg_bx[j], rg_lam[j])
            mixed = _merge(hml, hrg, p1, p2, ml_norm[j]).reshape(B * S, D)
            y = _matmul(mixed, ab_w_out, D, tn=512, layer=j, name="ab_out")
        else:
            wq = at_w_qkv.shape[2]
            p = _matmul(h, at_w_qkv, wq, tn=512, layer=j, name="at_qkv").reshape(B, S, wq)
            q, k, v = _qkprep(p, at_q_norm[j], at_k_norm[j], cos, sin)
            o = _attention(q, k, v).reshape(B * S, D)
            y = _matmul(o, at_w_o, D, tn=512, layer=j, name="at_out")
        xs, h2, logits = _resnorm(xs, y.reshape(B, S, D), norm_ffn[l], mt, wr_pad[l])
        pos, aff, post = _route(logits)
        xx, xc, gx, gc = _gather(h2, pos, aff)
        yx, yc = _ffn(xx, xc, gx, gc, moe_w_gate, moe_w_up, moe_w_down, l)
        xs = _scatter(xs, post, yx, yc, mt)
    return xs[:, n_ctx:]
```

```python
import functools
import math

import jax
import jax.numpy as jnp
from jax import lax
from jax.experimental import pallas as pl
from jax.experimental.pallas import tpu as pltpu

F32 = jnp.float32
BF16 = jnp.bfloat16
HIGHEST = lax.Precision.HIGHEST

NORM_EPS = 1e-6
GRID_W = 64
ROW_TILE = 256
LANES = 128
ML_HEADS = 8
ML_HEAD_DIM = 128
ML_WIDTH = ML_HEADS * ML_HEAD_DIM
N_GATES = 4 * ML_HEADS
RG_WIDTH = 1024
RG_BLOCKS = 8
RG_BLOCK_DIM = RG_WIDTH // RG_BLOCKS
RG_CONV = 4
RG_C = 8.0
AT_HEAD_DIM = 128
AT_HEADS = 16
AT_KV_HEADS = 4
AT_GROUP = AT_HEADS // AT_KV_HEADS
ROPE_AXIS_DIM = AT_HEAD_DIM // 2
ROPE_THETA = 10000.0
N_EXPERTS = 16
EC_FACTOR = 2
MM_ROWS = 2304
FFN_HIDDEN_TILE = 512
FFN_OUT_TILE = 512
HALO = 8
VMEM_LIMIT = 56 * 1024 * 1024


def _params(sem, vmem=VMEM_LIMIT):
    return pltpu.CompilerParams(dimension_semantics=sem, vmem_limit_bytes=vmem)


def _sigmoid(x):
    return 1.0 / (1.0 + jnp.exp(-x))


def _softplus(x):
    return jnp.maximum(x, 0.0) + jnp.log1p(jnp.exp(-jnp.abs(x)))


def _log_sigmoid(x):
    return -_softplus(-x)


def _mods_kernel(c_ref, w_ref, b_ref, o_ref):
    c = c_ref[...]
    a = c * _sigmoid(c)
    o_ref[0] = jnp.dot(a.astype(BF16), w_ref[0].astype(BF16), preferred_element_type=F32) + b_ref[0]


def _mods(c8, w_mod, b_mod):
    depth, d, n = w_mod.shape
    tn = 1024
    return pl.pallas_call(
        _mods_kernel,
        out_shape=jax.ShapeDtypeStruct((depth, 8, n), F32),
        grid=(depth, n // tn),
        in_specs=[pl.BlockSpec((8, d), lambda l, j: (0, 0)),
                  pl.BlockSpec((1, d, tn), lambda l, j: (l, 0, j)),
                  pl.BlockSpec((1, 1, tn), lambda l, j: (l, 0, j))],
        out_specs=pl.BlockSpec((1, 8, tn), lambda l, j: (l, 0, j)),
        compiler_params=_params(("parallel", "parallel")),
        name="mods",
    )(c8, w_mod, b_mod.reshape(depth, 1, n))


def _mod_spec(d):
    return pl.BlockSpec((1, 1, 6, d), lambda b, i: (b, jnp.minimum(i, 1), 0, 0))


def _rms(x, g):
    return x * lax.rsqrt(jnp.mean(x * x, axis=-1, keepdims=True) + NORM_EPS) * g


def _norm_kernel(x_ref, g_ref, m_ref, o_ref):
    m = m_ref[0, 0]
    y = _rms(x_ref[0], g_ref[...])
    o_ref[0] = (y * (1.0 + m[1:2]) + m[0:1]).astype(o_ref.dtype)


def _norm_mod(xs, g, mt):
    b, s, d = xs.shape
    return pl.pallas_call(
        _norm_kernel,
        out_shape=jax.ShapeDtypeStruct((b, s, d), BF16),
        grid=(b, s // ROW_TILE),
        in_specs=[pl.BlockSpec((1, ROW_TILE, d), lambda b, i: (b, i, 0)),
                  pl.BlockSpec((1, d), lambda b, i: (0, 0)),
                  _mod_spec(d)],
        out_specs=pl.BlockSpec((1, ROW_TILE, d), lambda b, i: (b, i, 0)),
        compiler_params=_params(("parallel", "parallel")),
        name="norm_mod",
    )(xs, g.reshape(1, d), mt)


def _mm_kernel(a_ref, w_ref, o_ref):
    o_ref[...] = jnp.dot(a_ref[...], w_ref[...].astype(BF16),
                         preferred_element_type=F32).astype(o_ref.dtype)


def _matmul(a, w, n, *, tn, layer=None, out_dtype=F32, name="matmul"):
    m, k = a.shape
    tm = math.gcd(m, MM_ROWS)
    if layer is None:
        wspec = pl.BlockSpec((k, tn), lambda i, j: (0, j))
    else:
        wspec = pl.BlockSpec((None, k, tn), lambda i, j: (layer, 0, j))
    return pl.pallas_call(
        _mm_kernel,
        out_shape=jax.ShapeDtypeStruct((m, n), out_dtype),
        grid=(m // tm, n // tn),
        in_specs=[pl.BlockSpec((tm, k), lambda i, j: (i, 0)), wspec],
        out_specs=pl.BlockSpec((tm, tn), lambda i, j: (i, j)),
        compiler_params=_params(("parallel", "parallel")),
        name=name,
    )(a, w)


def _scan_chunk(c, n_chunks, rev):
    return jnp.where(jnp.logical_or(c == 0, rev == 0), c, n_chunks - c)


def _mlstm_head(rev, head, vis, gc, gr, lfr, bc, br, q_ref, k_ref, v_ref, o_ref, c_ref, n_ref, m_ref):
    Dh = ML_HEAD_DIM
    blk = slice(head * Dh, (head + 1) * Dh)
    gi = int(rev) * 2 * ML_HEADS + head
    gf = gi + ML_HEADS
    nt = (((1,), (1,)), ((), ()))
    i_row = gr[gi:gi + 1, :]
    f_row = lfr[gf:gf + 1, :]
    b_row = br[gf:gf + 1, :]
    src = gc[:, gi:gi + 1] - bc[:, gf:gf + 1]
    d = jnp.where(vis, b_row + src, -jnp.inf)
    m_prev = m_ref[head:head + 1, :]
    inter = b_row + m_prev
    m_t = jnp.maximum(inter, jnp.max(d, axis=0, keepdims=True))
    w_inter = jnp.exp(inter - m_t)

    qb = q_ref[0, :, blk]
    kb = k_ref[0, :, blk]
    vt = v_ref[0, :, blk].astype(F32).T
    c = c_ref[head]
    n_row = n_ref[head:head + 1, :]
    n8 = jnp.broadcast_to(n_row, (8, Dh)).astype(BF16)
    log_scale = -0.5 * math.log(Dh)
    s = lax.dot_general(kb, qb, nt, preferred_element_type=F32) * jnp.exp(d - (m_t - log_scale))
    num = (w_inter * lax.dot_general(c.astype(BF16), qb, nt, preferred_element_type=F32)
           + jnp.dot(vt.astype(BF16), s.astype(BF16), preferred_element_type=F32))
    den = (w_inter * lax.dot_general(n8, qb, nt, preferred_element_type=F32)[0:1]
           + jnp.sum(s, axis=0, keepdims=True))
    o_ref[0, 0, :, blk] = (num / jnp.maximum(jnp.abs(den), jnp.exp(-m_t))).T

    b_end = jnp.sum(f_row, axis=1, keepdims=True)
    g_row = b_end - b_row + i_row
    m_new = jnp.maximum(b_end + m_prev, jnp.max(g_row, axis=1, keepdims=True))
    w_old = jnp.exp(b_end + m_prev - m_new)
    w_tok = jnp.exp(g_row - m_new) * (Dh ** -0.5)
    c_ref[head] = w_old * c + jnp.dot((vt * w_tok).astype(BF16), kb, preferred_element_type=F32)
    w8 = jnp.broadcast_to(w_tok, (8, w_tok.shape[1])).astype(BF16)
    n_ref[head:head + 1, :] = w_old * n_row + jnp.dot(w8, kb, preferred_element_type=F32)[0:1]
    m_ref[head:head + 1, :] = m_new


def _mlstm_kernel(q_ref, k_ref, v_ref, gc_ref, gr_ref, o_ref, c_ref, n_ref, m_ref):
    L = ROW_TILE

    @pl.when(pl.program_id(2) == 0)
    def _():
        c_ref[...] = jnp.zeros_like(c_ref)
        n_ref[...] = jnp.zeros_like(n_ref)
        m_ref[...] = jnp.zeros_like(m_ref)

    row = lax.broadcasted_iota(jnp.int32, (L, L), 0)
    col = lax.broadcasted_iota(jnp.int32, (L, L), 1)
    gc = gc_ref[0]
    gr = gr_ref[0]
    lfc = _log_sigmoid(gc)
    lfr = _log_sigmoid(gr)
    for rev in (False, True):
        @pl.when(pl.program_id(1) == int(rev))
        def _():
            vis_qs = (col >= row) if rev else (col <= row)
            vis_sq = (row >= col) if rev else (row <= col)
            bc = jnp.dot(jnp.where(vis_qs, 1.0, 0.0), lfc, preferred_element_type=F32, precision=HIGHEST)
            br = jnp.dot(lfr, jnp.where(vis_sq, 1.0, 0.0), preferred_element_type=F32, precision=HIGHEST)
            for head in range(ML_HEADS):
                _mlstm_head(rev, head, vis_sq, gc, gr, lfr, bc, br, q_ref, k_ref, v_ref, o_ref,
                            c_ref, n_ref, m_ref)


def _mlstm(p1, gcol, grow):
    b, s, _ = p1.shape
    nch = s // ROW_TILE
    L = ROW_TILE
    W = ML_WIDTH

    def qkv_spec(off):
        return pl.BlockSpec((1, L, W), lambda b, r, c: (b, _scan_chunk(c, nch, r), off))

    return pl.pallas_call(
        _mlstm_kernel,
        out_shape=jax.ShapeDtypeStruct((2, b, s, W), F32),
        grid=(b, 2, nch),
        in_specs=[qkv_spec(0), qkv_spec(1), qkv_spec(2),
                  pl.BlockSpec((1, L, N_GATES), lambda b, r, c: (b, _scan_chunk(c, nch, r), 0)),
                  pl.BlockSpec((1, N_GATES, L), lambda b, r, c: (b, 0, _scan_chunk(c, nch, r)))],
        out_specs=pl.BlockSpec((1, 1, L, W), lambda b, r, c: (r, b, _scan_chunk(c, nch, r), 0)),
        scratch_shapes=[pltpu.VMEM((ML_HEADS, ML_HEAD_DIM, ML_HEAD_DIM), F32),
                        pltpu.VMEM((ML_HEADS, ML_HEAD_DIM), F32),
                        pltpu.VMEM((ML_HEADS, 1), F32)],
        compiler_params=_params(("parallel", "parallel", "arbitrary")),
        name="mlstm",
    )(p1, p1, p1, gcol, grow)


def _rglru_chunk(rev, x_ref, cw_ref, cb_ref, wa_ref, ba_ref, wx_ref, bx_ref, lam_ref, o_ref,
                 ext_ref, a_ref, u_ref, h_ref):
    L = ROW_TILE
    c = pl.program_id(2)
    halo = slice(HALO + L, 2 * HALO + L) if rev else slice(0, HALO)

    @pl.when(c <= 1)
    def _():
        ext_ref[halo, :] = jnp.zeros((HALO, RG_WIDTH), F32)

    @pl.when(c == 0)
    def _():
        h_ref[...] = jnp.zeros_like(h_ref)

    x = x_ref[0]
    ext_ref[HALO:HALO + L, :] = x
    xc = cb_ref[0]
    for j in range(RG_CONV):
        k = RG_CONV - 1 - j
        lo = HALO + k if rev else HALO - k
        xc = xc + cw_ref[0, j:j + 1, :] * ext_ref[lo:lo + L, :]
    ext_ref[halo, :] = x[0:HALO] if rev else x[L - HALO:L]

    sp = _softplus(-lam_ref[0])
    xcb = xc.astype(BF16)
    for n in range(RG_BLOCKS):
        blk = slice(n * RG_BLOCK_DIM, (n + 1) * RG_BLOCK_DIM)
        xb = xcb[:, blk]
        r = _sigmoid(jnp.dot(xb, wa_ref[0, n].astype(BF16), preferred_element_type=F32)
                     + ba_ref[0, :, blk])
        i = _sigmoid(jnp.dot(xb, wx_ref[0, n].astype(BF16), preferred_element_type=F32)
                     + bx_ref[0, :, blk])
        log_a = -RG_C * r * sp[:, blk]
        a_ref[:, blk] = jnp.exp(log_a)
        th = jnp.tanh(log_a)
        u_ref[:, blk] = jnp.sqrt(-2.0 * th / (1.0 - th)) * (i * xc[:, blk])

    def step(t, h):
        tt = (L - 1 - t) if rev else t
        h = a_ref[pl.ds(tt, 1), :] * h + u_ref[pl.ds(tt, 1), :]
        o_ref[0, 0, pl.ds(tt, 1), :] = h
        return h

    h_ref[...] = lax.fori_loop(0, L, step, h_ref[...], unroll=8)


def _rglru_kernel(*refs):
    for rev in (False, True):
        @pl.when(pl.program_id(1) == int(rev))
        def _():
            _rglru_chunk(rev, *refs)


def _rglru(p2, cw, cb, wa, ba, wx, bx, lam):
    b, s, _ = p2.shape
    nch = s // ROW_TILE
    L = ROW_TILE
    R = RG_WIDTH
    vec = lambda a: a.reshape(2, 1, R)
    vspec = pl.BlockSpec((1, 1, R), lambda b, r, c: (r, 0, 0))
    wspec = pl.BlockSpec((1, RG_BLOCKS, RG_BLOCK_DIM, RG_BLOCK_DIM), lambda b, r, c: (r, 0, 0, 0))
    return pl.pallas_call(
        _rglru_kernel,
        out_shape=jax.ShapeDtypeStruct((2, b, s, R), F32),
        grid=(b, 2, nch),
        in_specs=[pl.BlockSpec((1, L, R), lambda b, r, c: (b, _scan_chunk(c, nch, r), 0)),
                  pl.BlockSpec((1, RG_CONV, R), lambda b, r, c: (r, 0, 0)),
                  vspec, wspec, vspec, wspec, vspec, vspec],
        out_specs=pl.BlockSpec((1, 1, L, R), lambda b, r, c: (r, b, _scan_chunk(c, nch, r), 0)),
        scratch_shapes=[pltpu.VMEM((L + 2 * HALO, R), F32),
                        pltpu.VMEM((L, R), F32),
                        pltpu.VMEM((L, R), F32),
                        pltpu.VMEM((1, R), F32)],
        compiler_params=_params(("parallel", "parallel", "arbitrary")),
        name="rglru",
    )(p2, cw, vec(cb), wa, vec(ba), wx, vec(bx), vec(lam))


def _gelu_tanh(x):
    return x * (0.5 * (1.0 + jnp.tanh(math.sqrt(2.0 / math.pi) * (x + 0.044715 * (x * x * x)))))


def _merge_kernel(mf_ref, mr_ref, rf_ref, rr_ref, o_ref, ry_ref, g_ref, y_ref):
    ml = mf_ref[0, 0] + mr_ref[0, 0]
    o = o_ref[0]
    g = g_ref[...]
    for h in range(ML_HEADS):
        blk = slice(h * ML_HEAD_DIM, (h + 1) * ML_HEAD_DIM)
        y = _rms(ml[:, blk], g[:, blk]) * _sigmoid(o[:, blk].astype(F32))
        y_ref[0, :, blk] = y.astype(y_ref.dtype)
    rg = rf_ref[0, 0] + rr_ref[0, 0]
    y_ref[0, :, ML_WIDTH:] = (rg * _gelu_tanh(ry_ref[0])).astype(y_ref.dtype)


def _merge(hml, hrg, p1, p2, ml_norm):
    _, b, s, _ = hml.shape
    L = ROW_TILE
    W = ML_WIDTH
    dspec = lambda r: pl.BlockSpec((1, 1, L, W), lambda b, i: (r, b, i, 0))
    return pl.pallas_call(
        _merge_kernel,
        out_shape=jax.ShapeDtypeStruct((b, s, 2 * W), BF16),
        grid=(b, s // L),
        in_specs=[dspec(0), dspec(1), dspec(0), dspec(1),
                  pl.BlockSpec((1, L, W), lambda b, i: (b, i, 3)),
                  pl.BlockSpec((1, L, W), lambda b, i: (b, i, 1)),
                  pl.BlockSpec((1, W), lambda b, i: (0, 0))],
        out_specs=pl.BlockSpec((1, L, 2 * W), lambda b, i: (b, i, 0)),
        compiler_params=_params(("parallel", "parallel")),
        name="merge",
    )(hml, hml, hrg, hrg, p1, p2, ml_norm.reshape(1, W))


def _rope_tables(s_ctx, t):
    rows = t // GRID_W
    row_ids = jnp.repeat(jnp.arange(rows), GRID_W).astype(F32)
    col_ids = jnp.tile(jnp.arange(GRID_W), rows).astype(F32)
    inv = ROPE_THETA ** (-jnp.arange(0, ROPE_AXIS_DIM, 2, dtype=F32) / ROPE_AXIS_DIM)
    ang_r = row_ids[:, None] * inv
    ang_c = col_ids[:, None] * inv
    ang = jnp.concatenate([ang_r, ang_r, ang_c, ang_c], axis=-1)
    cos = jnp.concatenate([jnp.ones((s_ctx, AT_HEAD_DIM), F32), jnp.cos(ang)], axis=0)
    sin = jnp.concatenate([jnp.zeros((s_ctx, AT_HEAD_DIM), F32), jnp.sin(ang)], axis=0)
    return cos, sin


def _rope(x, cos, sin):
    half = ROPE_AXIS_DIM // 2
    lane = lax.broadcasted_iota(jnp.int32, x.shape, 1)
    first = (lane % ROPE_AXIS_DIM) < half
    rot = jnp.where(first, -pltpu.roll(x, AT_HEAD_DIM - half, 1), pltpu.roll(x, half, 1))
    return x * cos + rot * sin


def _qkprep_kernel(p_ref, qn_ref, kn_ref, cos_ref, sin_ref, q_ref, k_ref, v_ref):
    cos = cos_ref[...]
    sin = sin_ref[...]
    Dh = AT_HEAD_DIM
    for h in range(AT_HEADS):
        x = _rope(_rms(p_ref[0, :, h * Dh:(h + 1) * Dh], qn_ref[...]), cos, sin)
        q_ref[0, :, h * Dh:(h + 1) * Dh] = (x * (Dh ** -0.5)).astype(q_ref.dtype)
    for h in range(AT_KV_HEADS):
        off = (AT_HEADS + h) * Dh
        x = _rope(_rms(p_ref[0, :, off:off + Dh], kn_ref[...]), cos, sin)
        k_ref[0, :, h * Dh:(h + 1) * Dh] = x.astype(k_ref.dtype)
    for h in range(AT_KV_HEADS):
        off = (AT_HEADS + AT_KV_HEADS + h) * Dh
        v_ref[0, :, 2 * h * Dh:(2 * h + 1) * Dh] = p_ref[0, :, off:off + Dh].astype(v_ref.dtype)
        v_ref[0, :, (2 * h + 1) * Dh:(2 * h + 2) * Dh] = jnp.ones((p_ref.shape[1], Dh), v_ref.dtype)


def _qkprep(p, q_norm, k_norm, cos, sin):
    b, s, w = p.shape
    L = ROW_TILE
    Dh = AT_HEAD_DIM
    qd, kd = AT_HEADS * Dh, AT_KV_HEADS * Dh
    return pl.pallas_call(
        _qkprep_kernel,
        out_shape=(jax.ShapeDtypeStruct((b, s, qd), BF16),
                   jax.ShapeDtypeStruct((b, s, kd), BF16),
                   jax.ShapeDtypeStruct((b, s, 2 * kd), BF16)),
        grid=(b, s // L),
        in_specs=[pl.BlockSpec((1, L, w), lambda b, i: (b, i, 0)),
                  pl.BlockSpec((1, Dh), lambda b, i: (0, 0)),
                  pl.BlockSpec((1, Dh), lambda b, i: (0, 0)),
                  pl.BlockSpec((L, Dh), lambda b, i: (i, 0)),
                  pl.BlockSpec((L, Dh), lambda b, i: (i, 0))],
        out_specs=(pl.BlockSpec((1, L, qd), lambda b, i: (b, i, 0)),
                   pl.BlockSpec((1, L, kd), lambda b, i: (b, i, 0)),
                   pl.BlockSpec((1, L, 2 * kd), lambda b, i: (b, i, 0))),
        compiler_params=_params(("parallel", "parallel")),
        name="qkprep",
    )(p, q_norm.reshape(1, Dh), k_norm.reshape(1, Dh), cos, sin)


def _attend(q_ref, k, v, o_ref):
    Dh = AT_HEAD_DIM
    for g in range(AT_GROUP):
        q = q_ref[0, :, g * Dh:(g + 1) * Dh]
        s = lax.dot_general(q, k, (((1,), (1,)), ((), ())), preferred_element_type=F32)
        e = jnp.exp(s - jnp.max(s, axis=-1, keepdims=True)).astype(BF16)
        o = jnp.dot(e, v, preferred_element_type=F32)
        o_ref[0, :, g * Dh:(g + 1) * Dh] = (o[:, :Dh] / o[:, Dh:]).astype(o_ref.dtype)


def _attn_kernel(q_ref, k_ref, v_ref, o_ref):
    i = pl.program_id(2)

    @pl.when(i == 0)
    def _():
        _attend(q_ref, k_ref[0, 0:ROW_TILE, :], v_ref[0, 0:ROW_TILE, :], o_ref)

    @pl.when(i > 0)
    def _():
        _attend(q_ref, k_ref[0], v_ref[0], o_ref)


def _attention(q, k, v):
    b, s, qd = q.shape
    L = ROW_TILE
    Dh = AT_HEAD_DIM
    gw = AT_GROUP * Dh
    return pl.pallas_call(
        _attn_kernel,
        out_shape=jax.ShapeDtypeStruct((b, s, qd), BF16),
        grid=(b, AT_KV_HEADS, s // L),
        in_specs=[pl.BlockSpec((1, L, gw), lambda b, h, i: (b, i, h)),
                  pl.BlockSpec((1, s, Dh), lambda b, h, i: (b, 0, h)),
                  pl.BlockSpec((1, s, 2 * Dh), lambda b, h, i: (b, 0, h))],
        out_specs=pl.BlockSpec((1, L, gw), lambda b, h, i: (b, i, h)),
        compiler_params=_params(("parallel", "parallel", "parallel")),
        name="attention",
    )(q, k, v)


def _cast_kernel(w_ref, o_ref):
    o_ref[...] = w_ref[...].astype(o_ref.dtype)


def _cast_bf16(w, layer):
    _, k, n = w.shape
    tk = math.gcd(k, 512)
    return pl.pallas_call(
        _cast_kernel,
        out_shape=jax.ShapeDtypeStruct((k, n), BF16),
        grid=(k // tk,),
        in_specs=[pl.BlockSpec((None, tk, n), lambda i: (layer, i, 0))],
        out_specs=pl.BlockSpec((tk, n), lambda i: (i, 0)),
        compiler_params=_params(("parallel",)),
        name="cast_bf16",
    )(w)


def _outproj_kernel(a_ref, w_ref, x_ref, g_ref, m_ref, wr_ref, xo_ref, h_ref, lg_ref):
    m = m_ref[0, 0]
    y = jnp.dot(a_ref[0], w_ref[...], preferred_element_type=F32)
    x = x_ref[0] + m[2:3] * y
    xo_ref[0] = x
    h = _rms(x, g_ref[...]) * (1.0 + m[4:5]) + m[3:4]
    hb = h.astype(h_ref.dtype)
    h_ref[0] = hb
    w2 = wr_ref[...]
    h_lo = (h - hb.astype(F32)).astype(BF16)
    both = jnp.dot(hb, w2, preferred_element_type=F32)
    lg_ref[0] = (both[:, :LANES] + both[:, LANES:]
                 + jnp.dot(h_lo, w2[:, :LANES], preferred_element_type=F32))


def _split_router(w_router):
    w = jnp.pad(w_router, ((0, 0), (0, 0), (0, LANES - w_router.shape[2])))
    hi = w.astype(BF16)
    lo = (w - hi.astype(F32)).astype(BF16)
    return jnp.concatenate([hi, lo], axis=2)


def _outproj(a, w, xs, g, mt, wr):
    b, s, d = xs.shape
    L = ROW_TILE
    tile = pl.BlockSpec((1, L, d), lambda b, i: (b, i, 0))
    return pl.pallas_call(
        _outproj_kernel,
        out_shape=(jax.ShapeDtypeStruct((b, s, d), F32),
                   jax.ShapeDtypeStruct((b, s, d), BF16),
                   jax.ShapeDtypeStruct((b, s, LANES), F32)),
        grid=(b, s // L),
        in_specs=[tile, pl.BlockSpec((d, d), lambda b, i: (0, 0)), tile,
                  pl.BlockSpec((1, d), lambda b, i: (0, 0)), _mod_spec(d),
                  pl.BlockSpec((d, 2 * LANES), lambda b, i: (0, 0))],
        out_specs=(tile, tile, pl.BlockSpec((1, L, LANES), lambda b, i: (b, i, 0))),
        compiler_params=_params(("parallel", "parallel")),
        name="outproj",
    )(a, w, xs, g.reshape(1, d), mt, wr)


def _excl_prefix(mask):
    e, n = mask.shape
    groups = n // LANES
    r = lax.broadcasted_iota(jnp.int32, (LANES, LANES), 0)
    c = lax.broadcasted_iota(jnp.int32, (LANES, LANES), 1)
    upper = jnp.where(r < c, 1.0, 0.0).astype(BF16)
    stk = jnp.concatenate([mask[:, g * LANES:(g + 1) * LANES] for g in range(groups)], axis=0)
    within = jnp.dot(stk.astype(BF16), upper, preferred_element_type=F32)
    tot = jnp.sum(stk, axis=1, keepdims=True)
    outs = []
    off = jnp.zeros((e, 1), F32)
    for g in range(groups):
        outs.append(within[g * e:(g + 1) * e] + off)
        off = off + tot[g * e:(g + 1) * e]
    return jnp.concatenate(outs, axis=1)


def _top_slots(aff, cap):
    v = lax.bitcast_convert_type(aff, jnp.int32)
    thr = jnp.zeros((aff.shape[0], 1), jnp.int32)
    for bit in range(30, -1, -1):
        cand = thr | (1 << bit)
        cnt = jnp.sum(jnp.where(v >= cand, 1.0, 0.0), axis=1, keepdims=True)
        thr = jnp.where(cnt >= cap, cand, thr)
    gt = jnp.where(v > thr, 1.0, 0.0)
    eq = jnp.where(v == thr, 1.0, 0.0)
    need = cap - jnp.sum(gt, axis=1, keepdims=True)
    sel = gt + eq * jnp.where(_excl_prefix(eq) < need, 1.0, 0.0)
    return jnp.where(sel > 0.0, _excl_prefix(sel), -1.0)


def _route_kernel(lg_ref, pos_ref, aff_ref, post_ref):
    s = lg_ref.shape[1]
    E = N_EXPERTS
    lt = lg_ref[0].T[0:E, :]
    ex = jnp.exp(lt - jnp.max(lt, axis=0, keepdims=True))
    aff = ex / jnp.sum(ex, axis=0, keepdims=True)
    aff_ref[0] = aff
    pos = jnp.concatenate(
        [_top_slots(aff[:, :ROW_TILE], EC_FACTOR * ROW_TILE // E),
         _top_slots(aff[:, ROW_TILE:], EC_FACTOR * (s - ROW_TILE) // E)], axis=1)
    pos_ref[0] = pos
    padded = jnp.concatenate([pos, jnp.full((LANES - E, s), -1.0, F32)], axis=0)
    post_ref[0] = padded.T


def _route(logits):
    b, s, _ = logits.shape
    E = N_EXPERTS
    return pl.pallas_call(
        _route_kernel,
        out_shape=(jax.ShapeDtypeStruct((b, E, s), F32),
                   jax.ShapeDtypeStruct((b, E, s), F32),
                   jax.ShapeDtypeStruct((b, s, LANES), F32)),
        grid=(b,),
        in_specs=[pl.BlockSpec((1, s, LANES), lambda b: (b, 0, 0))],
        out_specs=(pl.BlockSpec((1, E, s), lambda b: (b, 0, 0)),
                   pl.BlockSpec((1, E, s), lambda b: (b, 0, 0)),
                   pl.BlockSpec((1, s, LANES), lambda b: (b, 0, 0))),
        compiler_params=_params(("parallel",)),
        name="route",
    )(logits)


def _gather_kernel(h_ref, pos_ref, aff_ref, xx_ref, xc_ref, gx_ref, gc_ref):
    s = h_ref.shape[1]
    for lo, n, x_ref, g_ref in ((0, ROW_TILE, xc_ref, gc_ref), (ROW_TILE, s - ROW_TILE, xx_ref, gx_ref)):
        cap = x_ref.shape[1]
        p = pos_ref[0, 0, :, lo:lo + n]
        a = aff_ref[0, 0, :, lo:lo + n]
        slot = lax.broadcasted_iota(jnp.int32, (cap, n), 0).astype(F32)
        hit = p == slot
        onehot = jnp.where(hit, 1.0, 0.0).astype(BF16)
        x_ref[0] = jnp.dot(onehot, h_ref[0, lo:lo + n, :], preferred_element_type=F32).astype(x_ref.dtype)
        g_ref[0] = jnp.sum(jnp.where(hit, a, 0.0), axis=1, keepdims=True)


def _gather(h, pos, aff):
    b, s, d = h.shape
    E = N_EXPERTS
    cap_c = EC_FACTOR * ROW_TILE // E
    cap_x = EC_FACTOR * (s - ROW_TILE) // E
    row = pl.BlockSpec((1, 1, 1, s), lambda b, e: (b, e, 0, 0))
    return pl.pallas_call(
        _gather_kernel,
        out_shape=(jax.ShapeDtypeStruct((E, b * cap_x, d), BF16),
                   jax.ShapeDtypeStruct((E, b * cap_c, d), BF16),
                   jax.ShapeDtypeStruct((E, b * cap_x, 1), F32),
                   jax.ShapeDtypeStruct((E, b * cap_c, 1), F32)),
        grid=(b, E),
        in_specs=[pl.BlockSpec((1, s, d), lambda b, e: (b, 0, 0)), row, row],
        out_specs=(pl.BlockSpec((1, cap_x, d), lambda b, e: (e, b, 0)),
                   pl.BlockSpec((1, cap_c, d), lambda b, e: (e, b, 0)),
                   pl.BlockSpec((1, cap_x, 1), lambda b, e: (e, b, 0)),
                   pl.BlockSpec((1, cap_c, 1), lambda b, e: (e, b, 0))),
        compiler_params=_params(("parallel", "arbitrary")),
        name="gather",
    )(h, pos.reshape(b, E, 1, s), aff.reshape(b, E, 1, s))


def _ffn_kernel(xx_ref, xc_ref, gx_ref, gc_ref, wg_ref, wu_ref, wd_ref, yx_ref, yc_ref, ax_ref, ac_ref):
    step = pl.program_id(1)
    nf = ax_ref.shape[0]
    pairs = ((xx_ref, ax_ref, gx_ref, yx_ref), (xc_ref, ac_ref, gc_ref, yc_ref))

    @pl.when(step < nf)
    def _():
        wg = wg_ref[0].astype(BF16)
        wu = wu_ref[0].astype(BF16)
        for x_ref, a_ref, _, _ in pairs:
            x = x_ref[0]
            g = jnp.dot(x, wg, preferred_element_type=F32)
            u = jnp.dot(x, wu, preferred_element_type=F32)
            a_ref[step] = (g * _sigmoid(g) * u).astype(BF16)

    @pl.when(step >= nf)
    def _():
        wd = wd_ref[0].astype(BF16)
        tf = ax_ref.shape[2]
        for _, a_ref, g_ref, y_ref in pairs:
            y = jnp.dot(a_ref[0], wd[0:tf], preferred_element_type=F32)
            for f in range(1, nf):
                y = y + jnp.dot(a_ref[f], wd[f * tf:(f + 1) * tf], preferred_element_type=F32)
            y_ref[0] = (y * g_ref[0]).astype(y_ref.dtype)


def _ffn(xx, xc, gx, gc, w_gate, w_up, w_down, layer):
    E, rx, d = xx.shape
    rc = xc.shape[1]
    ff = w_gate.shape[3]
    nf = ff // FFN_HIDDEN_TILE
    nd = d // FFN_OUT_TILE
    xspec = lambda r, w: pl.BlockSpec((1, r, w), lambda e, s: (e, 0, 0))
    hid = lambda e, s: (layer, e, 0, jnp.minimum(s, nf - 1))
    out_tile = lambda s: jnp.maximum(s - nf, 0)
    yspec = lambda r: pl.BlockSpec((1, r, FFN_OUT_TILE), lambda e, s: (e, 0, out_tile(s)))
    return pl.pallas_call(
        _ffn_kernel,
        out_shape=(jax.ShapeDtypeStruct((E, rx, d), BF16), jax.ShapeDtypeStruct((E, rc, d), BF16)),
        grid=(E, nf + nd),
        in_specs=[xspec(rx, d), xspec(rc, d), xspec(rx, 1), xspec(rc, 1),
                  pl.BlockSpec((None, 1, d, FFN_HIDDEN_TILE), hid),
                  pl.BlockSpec((None, 1, d, FFN_HIDDEN_TILE), hid),
                  pl.BlockSpec((None, 1, ff, FFN_OUT_TILE), lambda e, s: (layer, e, 0, out_tile(s)))],
        out_specs=(yspec(rx), yspec(rc)),
        scratch_shapes=[pltpu.VMEM((nf, rx, FFN_HIDDEN_TILE), BF16), pltpu.VMEM((nf, rc, FFN_HIDDEN_TILE), BF16)],
        compiler_params=_params(("parallel", "arbitrary")),
        name="expert_ffn",
    )(xx, xc, gx, gc, w_gate, w_up, w_down)


def _combine(pt, y_ref):
    cap = y_ref.shape[1]
    n = pt.shape[0]
    slot = lax.broadcasted_iota(jnp.int32, (n, cap), 1).astype(F32)
    acc = jnp.zeros((n, y_ref.shape[2]), F32)
    for e in range(N_EXPERTS):
        onehot = jnp.where(pt[:, e:e + 1] == slot, 1.0, 0.0).astype(BF16)
        acc = acc + jnp.dot(onehot, y_ref[e], preferred_element_type=F32)
    return acc


def _scatter_kernel(x_ref, pt_ref, yx_ref, yc_ref, m_ref, gn_ref, mn_ref, o_ref, h_ref):
    i = pl.program_id(1)
    gate = m_ref[0, 0][5:6]
    mn = mn_ref[0, 0]

    def finish(acc):
        x = x_ref[0] + gate * acc
        o_ref[0] = x
        h_ref[0] = (_rms(x, gn_ref[...]) * (1.0 + mn[1:2]) + mn[0:1]).astype(h_ref.dtype)

    @pl.when(i == 0)
    def _():
        finish(_combine(pt_ref[0], yc_ref))

    @pl.when(i > 0)
    def _():
        finish(_combine(pt_ref[0], yx_ref))


def _scatter(xs, post, yx, yc, mt, g_next, mt_next):
    b, s, d = xs.shape
    E = N_EXPERTS
    L = ROW_TILE
    cap_x = yx.shape[1] // b
    cap_c = yc.shape[1] // b
    tile = pl.BlockSpec((1, L, d), lambda b, i: (b, i, 0))
    return pl.pallas_call(
        _scatter_kernel,
        out_shape=(jax.ShapeDtypeStruct((b, s, d), F32), jax.ShapeDtypeStruct((b, s, d), BF16)),
        grid=(b, s // L),
        in_specs=[tile,
                  pl.BlockSpec((1, L, LANES), lambda b, i: (b, i, 0)),
                  pl.BlockSpec((E, cap_x, d), lambda b, i: (0, b, 0)),
                  pl.BlockSpec((E, cap_c, d), lambda b, i: (0, b, 0)),
                  _mod_spec(d),
                  pl.BlockSpec((1, d), lambda b, i: (0, 0)),
                  _mod_spec(d)],
        out_specs=(tile, tile),
        compiler_params=_params(("parallel", "arbitrary")),
        name="scatter",
    )(xs, post, yx, yc, mt, g_next.reshape(1, d), mt_next)


def _scatter_last_kernel(x_ref, pt_ref, yx_ref, m_ref, o_ref):
    o_ref[0] = x_ref[0] + m_ref[0, 0][5:6] * _combine(pt_ref[0], yx_ref)


def _scatter_last(xs, post, yx, mt):
    b, s, d = xs.shape
    E = N_EXPERTS
    L = ROW_TILE
    cap_x = yx.shape[1] // b
    return pl.pallas_call(
        _scatter_last_kernel,
        out_shape=jax.ShapeDtypeStruct((b, s - L, d), F32),
        grid=(b, s // L - 1),
        in_specs=[pl.BlockSpec((1, L, d), lambda b, i: (b, i + 1, 0)),
                  pl.BlockSpec((1, L, LANES), lambda b, i: (b, i + 1, 0)),
                  pl.BlockSpec((E, cap_x, d), lambda b, i: (0, b, 0)),
                  pl.BlockSpec((1, 1, 6, d), lambda b, i: (b, 1, 0, 0))],
        out_specs=pl.BlockSpec((1, L, d), lambda b, i: (b, i, 0)),
        compiler_params=_params(("parallel", "arbitrary")),
        name="scatter_last",
    )(xs, post, yx, mt)


def kernel(x, c, ctx, c_ctx, w_mod, b_mod, norm_mix, norm_ffn, ab_w_in, ab_gate_b, ml_norm,
           rg_conv_w, rg_conv_b, rg_wa, rg_ba, rg_wx, rg_bx, rg_lam, ab_w_out,
           at_w_qkv, at_q_norm, at_k_norm, at_w_o,
           moe_w_router, moe_w_gate, moe_w_up, moe_w_down):
    B, T, D = x.shape
    n_ctx = ctx.shape[1]
    assert n_ctx == ROW_TILE and T % ROW_TILE == 0 and B <= 7
    S = n_ctx + T
    depth = w_mod.shape[0]
    E = moe_w_router.shape[2]

    c8 = jnp.concatenate([c, c_ctx[None], jnp.zeros((7 - B, D), F32)], axis=0)
    modv = _mods(c8, w_mod, b_mod).reshape(depth, 8, 6, D)
    mod_c = jnp.broadcast_to(modv[:, B][:, None], (depth, B, 6, D))
    mtab = jnp.stack([mod_c, modv[:, :B]], axis=2)
    cos, sin = _rope_tables(n_ctx, T)
    wr2 = _split_router(moe_w_router)

    xs = jnp.concatenate([ctx, x], axis=1)
    h = _norm_mod(xs, norm_mix[0], mtab[0])
    for l in range(depth):
        j = l // 2
        mt = mtab[l]
        h = h.reshape(B * S, D)
        if l % 2 == 0:
            g0 = 4 * ML_WIDTH
            p1 = _matmul(h, ab_w_in, g0, tn=512, layer=j, out_dtype=BF16, name="ab_in_qkvo").reshape(B, S, g0)
            w_rg = ab_w_in[j, :, g0 + N_GATES:]
            p2 = _matmul(h, w_rg, 2 * RG_WIDTH, tn=512, name="ab_in_rg").reshape(B, S, 2 * RG_WIDTH)
            w_g = jnp.pad(ab_w_in[j, :, g0:g0 + N_GATES], ((0, 0), (0, LANES - N_GATES)))
            gates = _matmul(h, w_g, LANES, tn=LANES, name="ab_in_gates").reshape(B, S, LANES)
            gcol = gates[:, :, :N_GATES] + ab_gate_b[j].reshape(-1)
            hml = _mlstm(p1, gcol, jnp.swapaxes(gcol, 1, 2))
            hrg = _rglru(p2, rg_conv_w[j], rg_conv_b[j], rg_wa[j], rg_ba[j], rg_wx[j], rg_bx[j], rg_lam[j])
            mixed = _merge(hml, hrg, p1, p2, ml_norm[j])
            w_out = _cast_bf16(ab_w_out, j)
        else:
            wq = at_w_qkv.shape[2]
            p = _matmul(h, at_w_qkv, wq, tn=512, layer=j, name="at_qkv").reshape(B, S, wq)
            q, k, v = _qkprep(p, at_q_norm[j], at_k_norm[j], cos, sin)
            mixed = _attention(q, k, v)
            w_out = _cast_bf16(at_w_o, j)
        xs, h2, logits = _outproj(mixed, w_out, xs, norm_ffn[l], mt, wr2[l])
        pos, aff, post = _route(logits)
        xx, xc, gx, gc = _gather(h2, pos, aff)
        yx, yc = _ffn(xx, xc, gx, gc, moe_w_gate, moe_w_up, moe_w_down, l)
        if l + 1 < depth:
            xs, h = _scatter(xs, post, yx, yc, mt, norm_mix[l + 1], mtab[l + 1])
        else:
            return _scatter_last(xs, post, yx, mt)
```

```python
import functools
import math

import jax
import jax.numpy as jnp
from jax import lax
from jax.experimental import pallas as pl
from jax.experimental.pallas import tpu as pltpu

F32 = jnp.float32
BF16 = jnp.bfloat16
HIGHEST = lax.Precision.HIGHEST

NORM_EPS = 1e-6
GRID_W = 64
ROW_TILE = 256
LANES = 128
ML_HEADS = 8
ML_HEAD_DIM = 128
ML_WIDTH = ML_HEADS * ML_HEAD_DIM
N_GATES = 4 * ML_HEADS
RG_WIDTH = 1024
RG_BLOCKS = 8
RG_BLOCK_DIM = RG_WIDTH // RG_BLOCKS
RG_CONV = 4
RG_C = 8.0
AT_HEAD_DIM = 128
AT_HEADS = 16
AT_KV_HEADS = 4
AT_GROUP = AT_HEADS // AT_KV_HEADS
ROPE_AXIS_DIM = AT_HEAD_DIM // 2
ROPE_THETA = 10000.0
N_EXPERTS = 16
EC_FACTOR = 2
MM_ROWS = 2304
FFN_HIDDEN_TILE = 512
FFN_OUT_TILE = 512
GATHER_GROUP = 2
SCATTER_ROWS = 512
HALO = 8
VMEM_LIMIT = 56 * 1024 * 1024


def _params(sem, vmem=VMEM_LIMIT):
    return pltpu.CompilerParams(dimension_semantics=sem, vmem_limit_bytes=vmem)


def _sigmoid(x):
    return 0.5 * jnp.tanh(0.5 * x) + 0.5


def _softplus(x):
    return jnp.maximum(x, 0.0) + jnp.log1p(jnp.exp(-jnp.abs(x)))


def _log_sigmoid(x):
    return -_softplus(-x)


def _mods_kernel(c_ref, w_ref, b_ref, o_ref):
    c = c_ref[...]
    a = c * _sigmoid(c)
    o_ref[0] = jnp.dot(a.astype(BF16), w_ref[0].astype(BF16), preferred_element_type=F32) + b_ref[0]


def _mods(c8, w_mod, b_mod):
    depth, d, n = w_mod.shape
    tn = 1024
    return pl.pallas_call(
        _mods_kernel,
        out_shape=jax.ShapeDtypeStruct((depth, 8, n), F32),
        grid=(depth, n // tn),
        in_specs=[pl.BlockSpec((8, d), lambda l, j: (0, 0)),
                  pl.BlockSpec((1, d, tn), lambda l, j: (l, 0, j)),
                  pl.BlockSpec((1, 1, tn), lambda l, j: (l, 0, j))],
        out_specs=pl.BlockSpec((1, 8, tn), lambda l, j: (l, 0, j)),
        compiler_params=_params(("parallel", "parallel")),
        name="mods",
    )(c8, w_mod, b_mod.reshape(depth, 1, n))


def _mod_spec(d):
    return pl.BlockSpec((1, 1, 6, d), lambda b, i: (b, jnp.minimum(i, 1), 0, 0))


def _rms(x, g):
    return x * lax.rsqrt(jnp.mean(x * x, axis=-1, keepdims=True) + NORM_EPS) * g


def _norm_kernel(c_ref, x_ref, g_ref, m_ref, xs_ref, h_ref):
    m = m_ref[0, 0]

    def emit(x):
        xs_ref[0] = x
        h_ref[0] = (_rms(x, g_ref[...]) * (1.0 + m[1:2]) + m[0:1]).astype(h_ref.dtype)

    @pl.when(pl.program_id(1) == 0)
    def _():
        emit(c_ref[0])

    @pl.when(pl.program_id(1) > 0)
    def _():
        emit(x_ref[0])


def _norm_mod(ctx, x, g, mt):
    b, t, d = x.shape
    L = ROW_TILE
    s = L + t
    tile = pl.BlockSpec((1, L, d), lambda b, i: (b, i, 0))
    return pl.pallas_call(
        _norm_kernel,
        out_shape=(jax.ShapeDtypeStruct((b, s, d), F32), jax.ShapeDtypeStruct((b, s, d), BF16)),
        grid=(b, s // L),
        in_specs=[pl.BlockSpec((1, L, d), lambda b, i: (b, 0, 0)),
                  pl.BlockSpec((1, L, d), lambda b, i: (b, jnp.maximum(i - 1, 0), 0)),
                  pl.BlockSpec((1, d), lambda b, i: (0, 0)),
                  _mod_spec(d)],
        out_specs=(tile, tile),
        compiler_params=_params(("parallel", "arbitrary")),
        name="norm_mod",
    )(ctx, x, g.reshape(1, d), mt)


def _mm_kernel(a_ref, w_ref, o_ref):
    o_ref[...] = jnp.dot(a_ref[...], w_ref[...].astype(BF16),
                         preferred_element_type=F32).astype(o_ref.dtype)


def _matmul(a, w, n, *, tn, layer=None, out_dtype=F32, name="matmul"):
    m, k = a.shape
    tm = math.gcd(m, MM_ROWS)
    if layer is None:
        wspec = pl.BlockSpec((k, tn), lambda i, j: (0, j))
    else:
        wspec = pl.BlockSpec((None, k, tn), lambda i, j: (layer, 0, j))
    return pl.pallas_call(
        _mm_kernel,
        out_shape=jax.ShapeDtypeStruct((m, n), out_dtype),
        grid=(m // tm, n // tn),
        in_specs=[pl.BlockSpec((tm, k), lambda i, j: (i, 0)), wspec],
        out_specs=pl.BlockSpec((tm, tn), lambda i, j: (i, j)),
        compiler_params=_params(("parallel", "parallel")),
        name=name,
    )(a, w)


def _scan_chunk(c, n_chunks, rev):
    return jnp.where(jnp.logical_or(c == 0, rev == 0), c, n_chunks - c)


def _mlstm_head(rev, head, vis, gc, gr, lfr, bc, br, q_ref, k_ref, v_ref, o_ref, c_ref, n_ref, m_ref):
    Dh = ML_HEAD_DIM
    blk = slice(head * Dh, (head + 1) * Dh)
    gi = int(rev) * 2 * ML_HEADS + head
    gf = gi + ML_HEADS
    nt = (((1,), (1,)), ((), ()))
    i_row = gr[gi:gi + 1, :]
    f_row = lfr[gf:gf + 1, :]
    b_row = br[gf:gf + 1, :]
    src = gc[:, gi:gi + 1] - bc[:, gf:gf + 1]
    d = jnp.where(vis, b_row + src, -jnp.inf)
    m_prev = m_ref[head:head + 1, :]
    inter = b_row + m_prev
    m_t = jnp.maximum(inter, jnp.max(d, axis=0, keepdims=True))
    w_inter = jnp.exp(inter - m_t)

    qb = q_ref[0, :, blk]
    kb = k_ref[0, :, blk]
    vt = v_ref[0, :, blk].astype(F32).T
    c = c_ref[head]
    n_row = n_ref[head:head + 1, :]
    n8 = jnp.broadcast_to(n_row, (8, Dh)).astype(BF16)
    log_scale = -0.5 * math.log(Dh)
    s = lax.dot_general(kb, qb, nt, preferred_element_type=F32) * jnp.exp(d - (m_t - log_scale))
    num = (w_inter * lax.dot_general(c.astype(BF16), qb, nt, preferred_element_type=F32)
           + jnp.dot(vt.astype(BF16), s.astype(BF16), preferred_element_type=F32))
    den = (w_inter * lax.dot_general(n8, qb, nt, preferred_element_type=F32)[0:1]
           + jnp.sum(s, axis=0, keepdims=True))
    o_ref[0, 0, :, blk] = (num / jnp.maximum(jnp.abs(den), jnp.exp(-m_t))).T

    b_end = jnp.sum(f_row, axis=1, keepdims=True)
    g_row = b_end - b_row + i_row
    m_new = jnp.maximum(b_end + m_prev, jnp.max(g_row, axis=1, keepdims=True))
    w_old = jnp.exp(b_end + m_prev - m_new)
    w_tok = jnp.exp(g_row - m_new) * (Dh ** -0.5)
    c_ref[head] = w_old * c + jnp.dot((vt * w_tok).astype(BF16), kb, preferred_element_type=F32)
    w8 = jnp.broadcast_to(w_tok, (8, w_tok.shape[1])).astype(BF16)
    n_ref[head:head + 1, :] = w_old * n_row + jnp.dot(w8, kb, preferred_element_type=F32)[0:1]
    m_ref[head:head + 1, :] = m_new


def _mlstm_kernel(q_ref, k_ref, v_ref, gc_ref, gr_ref, o_ref, c_ref, n_ref, m_ref):
    L = ROW_TILE

    @pl.when(pl.program_id(2) == 0)
    def _():
        c_ref[...] = jnp.zeros_like(c_ref)
        n_ref[...] = jnp.zeros_like(n_ref)
        m_ref[...] = jnp.zeros_like(m_ref)

    row = lax.broadcasted_iota(jnp.int32, (L, L), 0)
    col = lax.broadcasted_iota(jnp.int32, (L, L), 1)
    gc = gc_ref[0]
    gr = gr_ref[0]
    lfc = _log_sigmoid(gc)
    lfr = _log_sigmoid(gr)
    for rev in (False, True):
        @pl.when(pl.program_id(1) == int(rev))
        def _():
            vis_qs = (col >= row) if rev else (col <= row)
            vis_sq = (row >= col) if rev else (row <= col)
            bc = jnp.dot(jnp.where(vis_qs, 1.0, 0.0), lfc, preferred_element_type=F32, precision=HIGHEST)
            br = jnp.dot(lfr, jnp.where(vis_sq, 1.0, 0.0), preferred_element_type=F32, precision=HIGHEST)
            for head in range(ML_HEADS):
                _mlstm_head(rev, head, vis_sq, gc, gr, lfr, bc, br, q_ref, k_ref, v_ref, o_ref,
                            c_ref, n_ref, m_ref)


def _mlstm(p1, gcol, grow):
    b, s, _ = p1.shape
    nch = s // ROW_TILE
    L = ROW_TILE
    W = ML_WIDTH

    def qkv_spec(off):
        return pl.BlockSpec((1, L, W), lambda b, r, c: (b, _scan_chunk(c, nch, r), off))

    return pl.pallas_call(
        _mlstm_kernel,
        out_shape=jax.ShapeDtypeStruct((2, b, s, W), F32),
        grid=(b, 2, nch),
        in_specs=[qkv_spec(0), qkv_spec(1), qkv_spec(2),
                  pl.BlockSpec((1, L, N_GATES), lambda b, r, c: (b, _scan_chunk(c, nch, r), 0)),
                  pl.BlockSpec((1, N_GATES, L), lambda b, r, c: (b, 0, _scan_chunk(c, nch, r)))],
        out_specs=pl.BlockSpec((1, 1, L, W), lambda b, r, c: (r, b, _scan_chunk(c, nch, r), 0)),
        scratch_shapes=[pltpu.VMEM((ML_HEADS, ML_HEAD_DIM, ML_HEAD_DIM), F32),
                        pltpu.VMEM((ML_HEADS, ML_HEAD_DIM), F32),
                        pltpu.VMEM((ML_HEADS, 1), F32)],
        compiler_params=_params(("parallel", "parallel", "arbitrary")),
        name="mlstm",
    )(p1, p1, p1, gcol, grow)


def _rglru_chunk(rev, x_ref, cw_ref, cb_ref, wa_ref, ba_ref, wx_ref, bx_ref, lam_ref, o_ref,
                 ext_ref, a_ref, u_ref, h_ref):
    L = ROW_TILE
    c = pl.program_id(2)
    halo = slice(HALO + L, 2 * HALO + L) if rev else slice(0, HALO)

    @pl.when(c <= 1)
    def _():
        ext_ref[halo, :] = jnp.zeros((HALO, RG_WIDTH), F32)

    @pl.when(c == 0)
    def _():
        h_ref[...] = jnp.zeros_like(h_ref)

    x = x_ref[0]
    ext_ref[HALO:HALO + L, :] = x
    xc = cb_ref[0]
    for j in range(RG_CONV):
        k = RG_CONV - 1 - j
        lo = HALO + k if rev else HALO - k
        xc = xc + cw_ref[0, j:j + 1, :] * ext_ref[lo:lo + L, :]
    ext_ref[halo, :] = x[0:HALO] if rev else x[L - HALO:L]

    sp = _softplus(-lam_ref[0])
    xcb = xc.astype(BF16)
    for n in range(RG_BLOCKS):
        blk = slice(n * RG_BLOCK_DIM, (n + 1) * RG_BLOCK_DIM)
        xb = xcb[:, blk]
        r = _sigmoid(jnp.dot(xb, wa_ref[0, n].astype(BF16), preferred_element_type=F32)
                     + ba_ref[0, :, blk])
        i = _sigmoid(jnp.dot(xb, wx_ref[0, n].astype(BF16), preferred_element_type=F32)
                     + bx_ref[0, :, blk])
        log_a = -RG_C * r * sp[:, blk]
        a = jnp.exp(log_a)
        a_ref[:, blk] = a
        u_ref[:, blk] = jnp.sqrt(1.0 - a * a) * (i * xc[:, blk])

    def step(t, h):
        tt = (L - 1 - t) if rev else t
        h = a_ref[pl.ds(tt, 1), :] * h + u_ref[pl.ds(tt, 1), :]
        o_ref[0, 0, pl.ds(tt, 1), :] = h
        return h

    h_ref[...] = lax.fori_loop(0, L, step, h_ref[...], unroll=8)


def _rglru_kernel(*refs):
    for rev in (False, True):
        @pl.when(pl.program_id(1) == int(rev))
        def _():
            _rglru_chunk(rev, *refs)


def _rglru(p2, cw, cb, wa, ba, wx, bx, lam):
    b, s, _ = p2.shape
    nch = s // ROW_TILE
    L = ROW_TILE
    R = RG_WIDTH
    vec = lambda a: a.reshape(2, 1, R)
    vspec = pl.BlockSpec((1, 1, R), lambda b, r, c: (r, 0, 0))
    wspec = pl.BlockSpec((1, RG_BLOCKS, RG_BLOCK_DIM, RG_BLOCK_DIM), lambda b, r, c: (r, 0, 0, 0))
    return pl.pallas_call(
        _rglru_kernel,
        out_shape=jax.ShapeDtypeStruct((2, b, s, R), F32),
        grid=(b, 2, nch),
        in_specs=[pl.BlockSpec((1, L, R), lambda b, r, c: (b, _scan_chunk(c, nch, r), 0)),
                  pl.BlockSpec((1, RG_CONV, R), lambda b, r, c: (r, 0, 0)),
                  vspec, wspec, vspec, wspec, vspec, vspec],
        out_specs=pl.BlockSpec((1, 1, L, R), lambda b, r, c: (r, b, _scan_chunk(c, nch, r), 0)),
        scratch_shapes=[pltpu.VMEM((L + 2 * HALO, R), F32),
                        pltpu.VMEM((L, R), F32),
                        pltpu.VMEM((L, R), F32),
                        pltpu.VMEM((1, R), F32)],
        compiler_params=_params(("parallel", "parallel", "arbitrary")),
        name="rglru",
    )(p2, cw, vec(cb), wa, vec(ba), wx, vec(bx), vec(lam))


def _gelu_tanh(x):
    return x * (0.5 * (1.0 + jnp.tanh(math.sqrt(2.0 / math.pi) * (x + 0.044715 * (x * x * x)))))


def _merge_kernel(mf_ref, mr_ref, rf_ref, rr_ref, o_ref, ry_ref, g_ref, y_ref):
    ml = mf_ref[0, 0] + mr_ref[0, 0]
    o = o_ref[0]
    g = g_ref[...]
    for h in range(ML_HEADS):
        blk = slice(h * ML_HEAD_DIM, (h + 1) * ML_HEAD_DIM)
        y = _rms(ml[:, blk], g[:, blk]) * _sigmoid(o[:, blk].astype(F32))
        y_ref[0, :, blk] = y.astype(y_ref.dtype)
    rg = rf_ref[0, 0] + rr_ref[0, 0]
    y_ref[0, :, ML_WIDTH:] = (rg * _gelu_tanh(ry_ref[0])).astype(y_ref.dtype)


def _merge(hml, hrg, p1, p2, ml_norm):
    _, b, s, _ = hml.shape
    L = ROW_TILE
    W = ML_WIDTH
    dspec = lambda r: pl.BlockSpec((1, 1, L, W), lambda b, i: (r, b, i, 0))
    return pl.pallas_call(
        _merge_kernel,
        out_shape=jax.ShapeDtypeStruct((b, s, 2 * W), BF16),
        grid=(b, s // L),
        in_specs=[dspec(0), dspec(1), dspec(0), dspec(1),
                  pl.BlockSpec((1, L, W), lambda b, i: (b, i, 3)),
                  pl.BlockSpec((1, L, W), lambda b, i: (b, i, 1)),
                  pl.BlockSpec((1, W), lambda b, i: (0, 0))],
        out_specs=pl.BlockSpec((1, L, 2 * W), lambda b, i: (b, i, 0)),
        compiler_params=_params(("parallel", "parallel")),
        name="merge",
    )(hml, hml, hrg, hrg, p1, p2, ml_norm.reshape(1, W))


def _rope_tables(s_ctx, t):
    rows = t // GRID_W
    row_ids = jnp.repeat(jnp.arange(rows), GRID_W).astype(F32)
    col_ids = jnp.tile(jnp.arange(GRID_W), rows).astype(F32)
    inv = ROPE_THETA ** (-jnp.arange(0, ROPE_AXIS_DIM, 2, dtype=F32) / ROPE_AXIS_DIM)
    ang_r = row_ids[:, None] * inv
    ang_c = col_ids[:, None] * inv
    ang = jnp.concatenate([ang_r, ang_r, ang_c, ang_c], axis=-1)
    cos = jnp.concatenate([jnp.ones((s_ctx, AT_HEAD_DIM), F32), jnp.cos(ang)], axis=0)
    sin = jnp.concatenate([jnp.zeros((s_ctx, AT_HEAD_DIM), F32), jnp.sin(ang)], axis=0)
    return cos, sin


def _rope(x, cos, sin):
    half = ROPE_AXIS_DIM // 2
    lane = lax.broadcasted_iota(jnp.int32, x.shape, 1)
    first = (lane % ROPE_AXIS_DIM) < half
    rot = jnp.where(first, -pltpu.roll(x, AT_HEAD_DIM - half, 1), pltpu.roll(x, half, 1))
    return x * cos + rot * sin


def _qkprep_kernel(p_ref, qn_ref, kn_ref, cos_ref, sin_ref, q_ref, k_ref, v_ref):
    cos = cos_ref[...]
    sin = sin_ref[...]
    Dh = AT_HEAD_DIM
    for h in range(AT_HEADS):
        x = _rope(_rms(p_ref[0, :, h * Dh:(h + 1) * Dh], qn_ref[...]), cos, sin)
        q_ref[0, :, h * Dh:(h + 1) * Dh] = (x * (Dh ** -0.5)).astype(q_ref.dtype)
    for h in range(AT_KV_HEADS):
        off = (AT_HEADS + h) * Dh
        x = _rope(_rms(p_ref[0, :, off:off + Dh], kn_ref[...]), cos, sin)
        k_ref[0, :, h * Dh:(h + 1) * Dh] = x.astype(k_ref.dtype)
    for h in range(AT_KV_HEADS):
        off = (AT_HEADS + AT_KV_HEADS + h) * Dh
        v_ref[0, :, 2 * h * Dh:(2 * h + 1) * Dh] = p_ref[0, :, off:off + Dh].astype(v_ref.dtype)
        v_ref[0, :, (2 * h + 1) * Dh:(2 * h + 2) * Dh] = jnp.ones((p_ref.shape[1], Dh), v_ref.dtype)


def _qkprep(p, q_norm, k_norm, cos, sin):
    b, s, w = p.shape
    L = ROW_TILE
    Dh = AT_HEAD_DIM
    qd, kd = AT_HEADS * Dh, AT_KV_HEADS * Dh
    return pl.pallas_call(
        _qkprep_kernel,
        out_shape=(jax.ShapeDtypeStruct((b, s, qd), BF16),
                   jax.ShapeDtypeStruct((b, s, kd), BF16),
                   jax.ShapeDtypeStruct((b, s, 2 * kd), BF16)),
        grid=(b, s // L),
        in_specs=[pl.BlockSpec((1, L, w), lambda b, i: (b, i, 0)),
                  pl.BlockSpec((1, Dh), lambda b, i: (0, 0)),
                  pl.BlockSpec((1, Dh), lambda b, i: (0, 0)),
                  pl.BlockSpec((L, Dh), lambda b, i: (i, 0)),
                  pl.BlockSpec((L, Dh), lambda b, i: (i, 0))],
        out_specs=(pl.BlockSpec((1, L, qd), lambda b, i: (b, i, 0)),
                   pl.BlockSpec((1, L, kd), lambda b, i: (b, i, 0)),
                   pl.BlockSpec((1, L, 2 * kd), lambda b, i: (b, i, 0))),
        compiler_params=_params(("parallel", "parallel")),
        name="qkprep",
    )(p, q_norm.reshape(1, Dh), k_norm.reshape(1, Dh), cos, sin)


def _attend(q_ref, k, v, o_ref):
    Dh = AT_HEAD_DIM
    for g in range(AT_GROUP):
        q = q_ref[0, :, g * Dh:(g + 1) * Dh]
        s = lax.dot_general(q, k, (((1,), (1,)), ((), ())), preferred_element_type=F32)
        e = jnp.exp((s - jnp.max(s, axis=-1, keepdims=True)).astype(BF16))
        o = jnp.dot(e, v, preferred_element_type=F32)
        o_ref[0, :, g * Dh:(g + 1) * Dh] = (o[:, :Dh] / o[:, Dh:]).astype(o_ref.dtype)


def _attn_kernel(q_ref, k_ref, v_ref, o_ref):
    i = pl.program_id(2)

    @pl.when(i == 0)
    def _():
        _attend(q_ref, k_ref[0, 0:ROW_TILE, :], v_ref[0, 0:ROW_TILE, :], o_ref)

    @pl.when(i > 0)
    def _():
        _attend(q_ref, k_ref[0], v_ref[0], o_ref)


def _attention(q, k, v):
    b, s, qd = q.shape
    L = ROW_TILE
    Dh = AT_HEAD_DIM
    gw = AT_GROUP * Dh
    return pl.pallas_call(
        _attn_kernel,
        out_shape=jax.ShapeDtypeStruct((b, s, qd), BF16),
        grid=(b, AT_KV_HEADS, s // L),
        in_specs=[pl.BlockSpec((1, L, gw), lambda b, h, i: (b, i, h)),
                  pl.BlockSpec((1, s, Dh), lambda b, h, i: (b, 0, h)),
                  pl.BlockSpec((1, s, 2 * Dh), lambda b, h, i: (b, 0, h))],
        out_specs=pl.BlockSpec((1, L, gw), lambda b, h, i: (b, i, h)),
        compiler_params=_params(("parallel", "parallel", "parallel")),
        name="attention",
    )(q, k, v)


def _cast_kernel(w_ref, o_ref):
    o_ref[...] = w_ref[...].astype(o_ref.dtype)


def _cast_bf16(w, layer):
    _, k, n = w.shape
    tk = math.gcd(k, 512)
    return pl.pallas_call(
        _cast_kernel,
        out_shape=jax.ShapeDtypeStruct((k, n), BF16),
        grid=(k // tk,),
        in_specs=[pl.BlockSpec((None, tk, n), lambda i: (layer, i, 0))],
        out_specs=pl.BlockSpec((tk, n), lambda i: (i, 0)),
        compiler_params=_params(("parallel",)),
        name="cast_bf16",
    )(w)


def _outproj_kernel(a_ref, w_ref, x_ref, g_ref, m_ref, wr_ref, xo_ref, h_ref, lg_ref):
    m = m_ref[0, 0]
    y = jnp.dot(a_ref[0], w_ref[...], preferred_element_type=F32)
    x = x_ref[0] + m[2:3] * y
    xo_ref[0] = x
    h = _rms(x, g_ref[...]) * (1.0 + m[4:5]) + m[3:4]
    hb = h.astype(h_ref.dtype)
    h_ref[0] = hb
    w2 = wr_ref[...]
    h_lo = (h - hb.astype(F32)).astype(BF16)
    both = jnp.dot(hb, w2, preferred_element_type=F32)
    lg_ref[0] = (both[:, :LANES] + both[:, LANES:]
                 + jnp.dot(h_lo, w2[:, :LANES], preferred_element_type=F32))


def _split_router(w_router):
    w = jnp.pad(w_router, ((0, 0), (0, 0), (0, LANES - w_router.shape[2])))
    hi = w.astype(BF16)
    lo = (w - hi.astype(F32)).astype(BF16)
    return jnp.concatenate([hi, lo], axis=2)


def _outproj(a, w, xs, g, mt, wr):
    b, s, d = xs.shape
    L = ROW_TILE
    tile = pl.BlockSpec((1, L, d), lambda b, i: (b, i, 0))
    return pl.pallas_call(
        _outproj_kernel,
        out_shape=(jax.ShapeDtypeStruct((b, s, d), F32),
                   jax.ShapeDtypeStruct((b, s, d), BF16),
                   jax.ShapeDtypeStruct((b, s, LANES), F32)),
        grid=(b, s // L),
        in_specs=[tile, pl.BlockSpec((d, d), lambda b, i: (0, 0)), tile,
                  pl.BlockSpec((1, d), lambda b, i: (0, 0)), _mod_spec(d),
                  pl.BlockSpec((d, 2 * LANES), lambda b, i: (0, 0))],
        out_specs=(tile, tile, pl.BlockSpec((1, L, LANES), lambda b, i: (b, i, 0))),
        compiler_params=_params(("parallel", "parallel")),
        name="outproj",
    )(a, w, xs, g.reshape(1, d), mt, wr)


def _excl_prefix(mask):
    e, n = mask.shape
    groups = n // LANES
    r = lax.broadcasted_iota(jnp.int32, (LANES, LANES), 0)
    c = lax.broadcasted_iota(jnp.int32, (LANES, LANES), 1)
    upper = jnp.where(r < c, 1.0, 0.0).astype(BF16)
    stk = jnp.concatenate([mask[:, g * LANES:(g + 1) * LANES] for g in range(groups)], axis=0)
    within = jnp.dot(stk.astype(BF16), upper, preferred_element_type=F32)
    tot = jnp.sum(stk, axis=1, keepdims=True)
    outs = []
    off = jnp.zeros((e, 1), F32)
    for g in range(groups):
        outs.append(within[g * e:(g + 1) * e] + off)
        off = off + tot[g * e:(g + 1) * e]
    return jnp.concatenate(outs, axis=1)


def _top_slots(aff, cap):
    v = lax.bitcast_convert_type(aff, jnp.int32)
    thr = jnp.zeros((aff.shape[0], 1), jnp.int32)
    for bit in range(30, -1, -1):
        cand = thr | (1 << bit)
        cnt = jnp.sum(jnp.where(v >= cand, 1.0, 0.0), axis=1, keepdims=True)
        thr = jnp.where(cnt >= cap, cand, thr)
    gt = jnp.where(v > thr, 1.0, 0.0)
    eq = jnp.where(v == thr, 1.0, 0.0)
    need = cap - jnp.sum(gt, axis=1, keepdims=True)
    sel = gt + eq * jnp.where(_excl_prefix(eq) < need, 1.0, 0.0)
    return jnp.where(sel > 0.0, _excl_prefix(sel), -1.0)


def _route_kernel(lg_ref, pos_ref, aff_ref, post_ref):
    s = lg_ref.shape[1]
    E = N_EXPERTS
    lt = lg_ref[0].T[0:E, :]
    ex = jnp.exp(lt - jnp.max(lt, axis=0, keepdims=True))
    aff = ex / jnp.sum(ex, axis=0, keepdims=True)
    aff_ref[0] = aff
    pos = jnp.concatenate(
        [_top_slots(aff[:, :ROW_TILE], EC_FACTOR * ROW_TILE // E),
         _top_slots(aff[:, ROW_TILE:], EC_FACTOR * (s - ROW_TILE) // E)], axis=1)
    pos_ref[0] = pos
    padded = jnp.concatenate([pos, jnp.full((LANES - E, s), -1.0, F32)], axis=0)
    post_ref[0] = padded.T


def _route(logits):
    b, s, _ = logits.shape
    E = N_EXPERTS
    return pl.pallas_call(
        _route_kernel,
        out_shape=(jax.ShapeDtypeStruct((b, E, s), F32),
                   jax.ShapeDtypeStruct((b, E, s), F32),
                   jax.ShapeDtypeStruct((b, s, LANES), F32)),
        grid=(b,),
        in_specs=[pl.BlockSpec((1, s, LANES), lambda b: (b, 0, 0))],
        out_specs=(pl.BlockSpec((1, E, s), lambda b: (b, 0, 0)),
                   pl.BlockSpec((1, E, s), lambda b: (b, 0, 0)),
                   pl.BlockSpec((1, s, LANES), lambda b: (b, 0, 0))),
        compiler_params=_params(("parallel",)),
        name="route",
    )(logits)


def _gather_kernel(h_ref, pos_ref, aff_ref, xx_ref, xc_ref, gx_ref, gc_ref):
    s = h_ref.shape[1]
    group = xx_ref.shape[0]
    for lo, n, x_ref, g_ref in ((0, ROW_TILE, xc_ref, gc_ref), (ROW_TILE, s - ROW_TILE, xx_ref, gx_ref)):
        cap = x_ref.shape[1]
        slot = lax.broadcasted_iota(jnp.int32, (cap, n), 0).astype(F32)
        hits = [pos_ref[0, j, :, lo:lo + n] == slot for j in range(group)]
        onehot = jnp.concatenate([jnp.where(hit, 1.0, 0.0).astype(BF16) for hit in hits], axis=0)
        rows = jnp.dot(onehot, h_ref[0, lo:lo + n, :], preferred_element_type=F32)
        for j in range(group):
            x_ref[j] = rows[j * cap:(j + 1) * cap].astype(x_ref.dtype)
            g_ref[j] = jnp.sum(jnp.where(hits[j], aff_ref[0, j, :, lo:lo + n], 0.0), axis=1, keepdims=True)


def _gather(h, pos, aff):
    b, s, d = h.shape
    E = N_EXPERTS
    G = GATHER_GROUP
    cap_c = EC_FACTOR * ROW_TILE // E
    cap_x = EC_FACTOR * (s - ROW_TILE) // E
    row = pl.BlockSpec((1, G, 1, s), lambda b, e: (b, e, 0, 0))
    return pl.pallas_call(
        _gather_kernel,
        out_shape=(jax.ShapeDtypeStruct((E, b * cap_x, d), BF16),
                   jax.ShapeDtypeStruct((E, b * cap_c, d), BF16),
                   jax.ShapeDtypeStruct((E, b * cap_x, 1), F32),
                   jax.ShapeDtypeStruct((E, b * cap_c, 1), F32)),
        grid=(b, E // G),
        in_specs=[pl.BlockSpec((1, s, d), lambda b, e: (b, 0, 0)), row, row],
        out_specs=(pl.BlockSpec((G, cap_x, d), lambda b, e: (e, b, 0)),
                   pl.BlockSpec((G, cap_c, d), lambda b, e: (e, b, 0)),
                   pl.BlockSpec((G, cap_x, 1), lambda b, e: (e, b, 0)),
                   pl.BlockSpec((G, cap_c, 1), lambda b, e: (e, b, 0))),
        compiler_params=_params(("parallel", "arbitrary")),
        name="gather",
    )(h, pos.reshape(b, E, 1, s), aff.reshape(b, E, 1, s))


def _ffn_kernel(xx_ref, xc_ref, gx_ref, gc_ref, wg_ref, wu_ref, wd_ref, yx_ref, yc_ref, ax_ref, ac_ref):
    step = pl.program_id(1)
    nf = ax_ref.shape[0]
    pairs = ((xx_ref, ax_ref, gx_ref, yx_ref), (xc_ref, ac_ref, gc_ref, yc_ref))

    @pl.when(step < nf)
    def _():
        wg = wg_ref[0].astype(BF16)
        wu = wu_ref[0].astype(BF16)
        for x_ref, a_ref, _, _ in pairs:
            x = x_ref[0]
            g = jnp.dot(x, wg, preferred_element_type=F32)
            u = jnp.dot(x, wu, preferred_element_type=F32)
            a_ref[step] = (g * _sigmoid(g) * u).astype(BF16)

    @pl.when(step >= nf)
    def _():
        wd = wd_ref[0].astype(BF16)
        tf = ax_ref.shape[2]
        for _, a_ref, g_ref, y_ref in pairs:
            y = jnp.dot(a_ref[0], wd[0:tf], preferred_element_type=F32)
            for f in range(1, nf):
                y = y + jnp.dot(a_ref[f], wd[f * tf:(f + 1) * tf], preferred_element_type=F32)
            y_ref[0] = (y * g_ref[0]).astype(y_ref.dtype)


def _ffn(xx, xc, gx, gc, w_gate, w_up, w_down, layer):
    E, rx, d = xx.shape
    rc = xc.shape[1]
    ff = w_gate.shape[3]
    nf = ff // FFN_HIDDEN_TILE
    nd = d // FFN_OUT_TILE
    xspec = lambda r, w: pl.BlockSpec((1, r, w), lambda e, s: (e, 0, 0))
    hid = lambda e, s: (layer, e, 0, jnp.minimum(s, nf - 1))
    out_tile = lambda s: jnp.maximum(s - nf, 0)
    yspec = lambda r: pl.BlockSpec((1, r, FFN_OUT_TILE), lambda e, s: (e, 0, out_tile(s)))
    return pl.pallas_call(
        _ffn_kernel,
        out_shape=(jax.ShapeDtypeStruct((E, rx, d), BF16), jax.ShapeDtypeStruct((E, rc, d), BF16)),
        grid=(E, nf + nd),
        in_specs=[xspec(rx, d), xspec(rc, d), xspec(rx, 1), xspec(rc, 1),
                  pl.BlockSpec((None, 1, d, FFN_HIDDEN_TILE), hid),
                  pl.BlockSpec((None, 1, d, FFN_HIDDEN_TILE), hid),
                  pl.BlockSpec((None, 1, ff, FFN_OUT_TILE), lambda e, s: (layer, e, 0, out_tile(s)))],
        out_specs=(yspec(rx), yspec(rc)),
        scratch_shapes=[pltpu.VMEM((nf, rx, FFN_HIDDEN_TILE), BF16), pltpu.VMEM((nf, rc, FFN_HIDDEN_TILE), BF16)],
        compiler_params=_params(("parallel", "arbitrary")),
        name="expert_ffn",
    )(xx, xc, gx, gc, w_gate, w_up, w_down)


def _combine(pt, y_ref):
    cap = y_ref.shape[1]
    n = pt.shape[0]
    slot = lax.broadcasted_iota(jnp.int32, (n, cap), 1).astype(F32)
    acc = jnp.zeros((n, y_ref.shape[2]), F32)
    for e in range(N_EXPERTS):
        onehot = jnp.where(pt[:, e:e + 1] == slot, 1.0, 0.0).astype(BF16)
        acc = acc + jnp.dot(onehot, y_ref[e], preferred_element_type=F32)
    return acc


def _scatter_kernel(with_norm, x_ref, pt_ref, y_ref, m_ref, *rest):
    x = x_ref[0] + m_ref[0, 0][5:6] * _combine(pt_ref[0], y_ref)
    if with_norm:
        gn_ref, mn_ref = rest[0], rest[1]
        o_ref, h_ref = rest[-2], rest[-1]
        mn = mn_ref[0, 0]
        h_ref[0] = (_rms(x, gn_ref[...]) * (1.0 + mn[1:2]) + mn[0:1]).astype(h_ref.dtype)
    else:
        o_ref = rest[-1]
    o_ref[0] = x


def _scatter_latent(xs, post, yx, mt, next_norm):
    b, s, d = xs.shape
    E = N_EXPERTS
    L = ROW_TILE
    rows = math.gcd(s - L, SCATTER_ROWS)
    cap_x = yx.shape[1] // b

    def row_spec(width):
        return pl.BlockSpec((pl.Element(1), pl.Element(rows), pl.Element(width)),
                            lambda b, i: (b, pl.multiple_of(L + rows * i, L), 0))

    latent_mods = pl.BlockSpec((1, 1, 6, d), lambda b, i: (b, 1, 0, 0))
    in_specs = [row_spec(d), row_spec(LANES),
                pl.BlockSpec((E, cap_x, d), lambda b, i: (0, b, 0), pipeline_mode=pl.Buffered(1)),
                latent_mods]
    args = [xs, post, yx, mt]
    if next_norm is None:
        out_shape = jax.ShapeDtypeStruct((b, s - L, d), F32)
        out_specs = pl.BlockSpec((1, rows, d), lambda b, i: (b, i, 0))
    else:
        in_specs += [pl.BlockSpec((1, d), lambda b, i: (0, 0)), latent_mods]
        args += [next_norm[0].reshape(1, d), next_norm[1]]
        out_shape = (jax.ShapeDtypeStruct((b, s, d), F32), jax.ShapeDtypeStruct((b, s, d), BF16))
        out_specs = (row_spec(d), row_spec(d))
    return pl.pallas_call(
        functools.partial(_scatter_kernel, next_norm is not None),
        out_shape=out_shape,
        grid=(b, (s - L) // rows),
        in_specs=in_specs,
        out_specs=out_specs,
        compiler_params=_params(("parallel", "arbitrary")),
        name="scatter_latent",
    )(*args)


def _scatter_context(xs, post, yc, mt, next_norm, xs_new, h_new):
    b, s, d = xs.shape
    E = N_EXPERTS
    L = ROW_TILE
    cap_c = yc.shape[1] // b
    tile = pl.BlockSpec((1, L, d), lambda b: (b, 0, 0))
    context_mods = pl.BlockSpec((1, 1, 6, d), lambda b: (b, 0, 0, 0))
    passthrough = pl.BlockSpec(memory_space=pl.ANY)

    def body(x_ref, pt_ref, y_ref, m_ref, gn_ref, mn_ref, _xs_new, _h_new, o_ref, h_ref):
        _scatter_kernel(True, x_ref, pt_ref, y_ref, m_ref, gn_ref, mn_ref, o_ref, h_ref)

    return pl.pallas_call(
        body,
        out_shape=(jax.ShapeDtypeStruct((b, s, d), F32), jax.ShapeDtypeStruct((b, s, d), BF16)),
        grid=(b,),
        in_specs=[tile, pl.BlockSpec((1, L, LANES), lambda b: (b, 0, 0)),
                  pl.BlockSpec((E, cap_c, d), lambda b: (0, b, 0)), context_mods,
                  pl.BlockSpec((1, d), lambda b: (0, 0)), context_mods, passthrough, passthrough],
        out_specs=(tile, tile),
        input_output_aliases={6: 0, 7: 1},
        compiler_params=_params(("parallel",)),
        name="scatter_context",
    )(xs, post, yc, mt, next_norm[0].reshape(1, d), next_norm[1], xs_new, h_new)


def kernel(x, c, ctx, c_ctx, w_mod, b_mod, norm_mix, norm_ffn, ab_w_in, ab_gate_b, ml_norm,
           rg_conv_w, rg_conv_b, rg_wa, rg_ba, rg_wx, rg_bx, rg_lam, ab_w_out,
           at_w_qkv, at_q_norm, at_k_norm, at_w_o,
           moe_w_router, moe_w_gate, moe_w_up, moe_w_down):
    B, T, D = x.shape
    n_ctx = ctx.shape[1]
    assert n_ctx == ROW_TILE and T % ROW_TILE == 0 and B <= 7
    S = n_ctx + T
    depth = w_mod.shape[0]
    E = moe_w_router.shape[2]

    c8 = jnp.concatenate([c, c_ctx[None], jnp.zeros((7 - B, D), F32)], axis=0)
    modv = _mods(c8, w_mod, b_mod).reshape(depth, 8, 6, D)
    mod_c = jnp.broadcast_to(modv[:, B][:, None], (depth, B, 6, D))
    mtab = jnp.stack([mod_c, modv[:, :B]], axis=2)
    cos, sin = _rope_tables(n_ctx, T)
    wr2 = _split_router(moe_w_router)

    xs, h = _norm_mod(ctx, x, norm_mix[0], mtab[0])
    for l in range(depth):
        j = l // 2
        mt = mtab[l]
        h = h.reshape(B * S, D)
        if l % 2 == 0:
            g0 = 4 * ML_WIDTH
            p1 = _matmul(h, ab_w_in, g0, tn=512, layer=j, out_dtype=BF16, name="ab_in_qkvo").reshape(B, S, g0)
            w_rg = ab_w_in[j, :, g0 + N_GATES:]
            p2 = _matmul(h, w_rg, 2 * RG_WIDTH, tn=512, name="ab_in_rg").reshape(B, S, 2 * RG_WIDTH)
            w_g = jnp.pad(ab_w_in[j, :, g0:g0 + N_GATES], ((0, 0), (0, LANES - N_GATES)))
            gates = _matmul(h, w_g, LANES, tn=LANES, name="ab_in_gates").reshape(B, S, LANES)
            gcol = gates[:, :, :N_GATES] + ab_gate_b[j].reshape(-1)
            hml = _mlstm(p1, gcol, jnp.swapaxes(gcol, 1, 2))
            hrg = _rglru(p2, rg_conv_w[j], rg_conv_b[j], rg_wa[j], rg_ba[j], rg_wx[j], rg_bx[j], rg_lam[j])
            mixed = _merge(hml, hrg, p1, p2, ml_norm[j])
            w_out = _cast_bf16(ab_w_out, j)
        else:
            wq = at_w_qkv.shape[2]
            p = _matmul(h, at_w_qkv, wq, tn=512, layer=j, name="at_qkv").reshape(B, S, wq)
            q, k, v = _qkprep(p, at_q_norm[j], at_k_norm[j], cos, sin)
            mixed = _attention(q, k, v)
            w_out = _cast_bf16(at_w_o, j)
        xs, h2, logits = _outproj(mixed, w_out, xs, norm_ffn[l], mt, wr2[l])
        pos, aff, post = _route(logits)
        xx, xc, gx, gc = _gather(h2, pos, aff)
        yx, yc = _ffn(xx, xc, gx, gc, moe_w_gate, moe_w_up, moe_w_down, l)
        if l + 1 == depth:
            return _scatter_latent(xs, post, yx, mt, None)
        next_norm = (norm_mix[l + 1], mtab[l + 1])
        xs_new, h_new = _scatter_latent(xs, post, yx, mt, next_norm)
        xs, h = _scatter_context(xs, post, yc, mt, next_norm, xs_new, h_new)
```

```python
import functools
import math

import jax
import jax.numpy as jnp
from jax import lax
from jax.experimental import pallas as pl
from jax.experimental.pallas import tpu as pltpu

F32 = jnp.float32
BF16 = jnp.bfloat16
HIGHEST = lax.Precision.HIGHEST

NORM_EPS = 1e-6
GRID_W = 64
ROW_TILE = 256
LANES = 128
ML_HEADS = 8
ML_HEAD_DIM = 128
ML_WIDTH = ML_HEADS * ML_HEAD_DIM
N_GATES = 4 * ML_HEADS
RG_WIDTH = 1024
RG_BLOCKS = 8
RG_BLOCK_DIM = RG_WIDTH // RG_BLOCKS
RG_CONV = 4
RG_C = 8.0
AT_HEAD_DIM = 128
AT_HEADS = 16
AT_KV_HEADS = 4
AT_GROUP = AT_HEADS // AT_KV_HEADS
ROPE_AXIS_DIM = AT_HEAD_DIM // 2
ROPE_THETA = 10000.0
N_EXPERTS = 16
EC_FACTOR = 2
MM_ROWS = 2304
FFN_HIDDEN_TILE = 512
FFN_OUT_TILE = 512
GATHER_GROUP = 2
HALO = 8
VMEM_LIMIT = 56 * 1024 * 1024


def _params(sem, vmem=VMEM_LIMIT):
    return pltpu.CompilerParams(dimension_semantics=sem, vmem_limit_bytes=vmem)


def _sigmoid(x):
    return 0.5 * jnp.tanh(0.5 * x) + 0.5


def _softplus(x):
    return jnp.maximum(x, 0.0) + jnp.log1p(jnp.exp(-jnp.abs(x)))


def _log_sigmoid(x):
    return -_softplus(-x)


def _mods_kernel(c_ref, w_ref, b_ref, o_ref):
    c = c_ref[...]
    a = c * _sigmoid(c)
    o_ref[0] = jnp.dot(a.astype(BF16), w_ref[0].astype(BF16), preferred_element_type=F32) + b_ref[0]


def _mods(c8, w_mod, b_mod):
    depth, d, n = w_mod.shape
    tn = 1024
    return pl.pallas_call(
        _mods_kernel,
        out_shape=jax.ShapeDtypeStruct((depth, 8, n), F32),
        grid=(depth, n // tn),
        in_specs=[pl.BlockSpec((8, d), lambda l, j: (0, 0)),
                  pl.BlockSpec((1, d, tn), lambda l, j: (l, 0, j)),
                  pl.BlockSpec((1, 1, tn), lambda l, j: (l, 0, j))],
        out_specs=pl.BlockSpec((1, 8, tn), lambda l, j: (l, 0, j)),
        compiler_params=_params(("parallel", "parallel")),
        name="mods",
    )(c8, w_mod, b_mod.reshape(depth, 1, n))


def _mod_spec(d):
    return pl.BlockSpec((1, 1, 6, d), lambda b, i: (b, jnp.minimum(i, 1), 0, 0))


def _rms(x, g):
    return x * lax.rsqrt(jnp.mean(x * x, axis=-1, keepdims=True) + NORM_EPS) * g


def _norm_kernel(c_ref, x_ref, g_ref, m_ref, xs_ref, h_ref):
    m = m_ref[0, 0]

    def emit(x):
        xs_ref[0] = x
        h_ref[0] = (_rms(x, g_ref[...]) * (1.0 + m[1:2]) + m[0:1]).astype(h_ref.dtype)

    @pl.when(pl.program_id(1) == 0)
    def _():
        emit(c_ref[0])

    @pl.when(pl.program_id(1) > 0)
    def _():
        emit(x_ref[0])


def _norm_mod(ctx, x, g, mt):
    b, t, d = x.shape
    L = ROW_TILE
    s = L + t
    tile = pl.BlockSpec((1, L, d), lambda b, i: (b, i, 0))
    return pl.pallas_call(
        _norm_kernel,
        out_shape=(jax.ShapeDtypeStruct((b, s, d), F32), jax.ShapeDtypeStruct((b, s, d), BF16)),
        grid=(b, s // L),
        in_specs=[pl.BlockSpec((1, L, d), lambda b, i: (b, 0, 0)),
                  pl.BlockSpec((1, L, d), lambda b, i: (b, jnp.maximum(i - 1, 0), 0)),
                  pl.BlockSpec((1, d), lambda b, i: (0, 0)),
                  _mod_spec(d)],
        out_specs=(tile, tile),
        compiler_params=_params(("parallel", "arbitrary")),
        name="norm_mod",
    )(ctx, x, g.reshape(1, d), mt)


def _mm_kernel(a_ref, w_ref, o_ref):
    o_ref[...] = jnp.dot(a_ref[...], w_ref[...].astype(BF16),
                         preferred_element_type=F32).astype(o_ref.dtype)


def _matmul(a, w, n, *, tn, layer=None, out_dtype=F32, name="matmul"):
    m, k = a.shape
    tm = math.gcd(m, MM_ROWS)
    if layer is None:
        wspec = pl.BlockSpec((k, tn), lambda i, j: (0, j))
    else:
        wspec = pl.BlockSpec((None, k, tn), lambda i, j: (layer, 0, j))
    return pl.pallas_call(
        _mm_kernel,
        out_shape=jax.ShapeDtypeStruct((m, n), out_dtype),
        grid=(m // tm, n // tn),
        in_specs=[pl.BlockSpec((tm, k), lambda i, j: (i, 0)), wspec],
        out_specs=pl.BlockSpec((tm, tn), lambda i, j: (i, j)),
        compiler_params=_params(("parallel", "parallel")),
        name=name,
    )(a, w)


def _mm_t_kernel(a_ref, w_ref, o_ref):
    o_ref[...] = lax.dot_general(a_ref[...], w_ref[0].astype(BF16), (((1,), (1,)), ((), ())),
                                 preferred_element_type=F32).astype(o_ref.dtype)


def _matmul_t(a, w_t, layer, row0, n, *, tn, out_dtype=F32, name="matmul_t"):
    m, k = a.shape
    tm = math.gcd(m, MM_ROWS)
    assert row0 % 8 == 0 and n % tn == 0
    wspec = pl.BlockSpec((pl.Element(1), pl.Element(tn), pl.Element(k)),
                         lambda i, j: (layer, pl.multiple_of(row0 + tn * j, 8), 0))
    return pl.pallas_call(
        _mm_t_kernel,
        out_shape=jax.ShapeDtypeStruct((m, n), out_dtype),
        grid=(m // tm, n // tn),
        in_specs=[pl.BlockSpec((tm, k), lambda i, j: (i, 0)), wspec],
        out_specs=pl.BlockSpec((tm, tn), lambda i, j: (i, j)),
        compiler_params=_params(("parallel", "parallel")),
        name=name,
    )(a, w_t)


def _scan_chunk(c, n_chunks, rev):
    return jnp.where(jnp.logical_or(c == 0, rev == 0), c, n_chunks - c)


def _mlstm_head(rev, head, vis, gc, gr, lfr, bc, br, q_ref, k_ref, v_ref, o_ref, c_ref, n_ref, m_ref):
    Dh = ML_HEAD_DIM
    blk = slice(head * Dh, (head + 1) * Dh)
    gi = int(rev) * 2 * ML_HEADS + head
    gf = gi + ML_HEADS
    nt = (((1,), (1,)), ((), ()))
    i_row = gr[gi:gi + 1, :]
    f_row = lfr[gf:gf + 1, :]
    b_row = br[gf:gf + 1, :]
    src = gc[:, gi:gi + 1] - bc[:, gf:gf + 1]
    d = jnp.where(vis, b_row + src, -jnp.inf)
    m_prev = m_ref[head:head + 1, :]
    inter = b_row + m_prev
    m_t = jnp.maximum(inter, jnp.max(d, axis=0, keepdims=True))
    w_inter = jnp.exp(inter - m_t)

    qb = q_ref[0, :, blk]
    kb = k_ref[0, :, blk]
    vt = v_ref[0, :, blk].astype(F32).T
    c = c_ref[head]
    n_row = n_ref[head:head + 1, :]
    n8 = jnp.broadcast_to(n_row, (8, Dh)).astype(BF16)
    log_scale = -0.5 * math.log(Dh)
    s = lax.dot_general(kb, qb, nt, preferred_element_type=F32) * jnp.exp(d - (m_t - log_scale))
    num = (w_inter * lax.dot_general(c.astype(BF16), qb, nt, preferred_element_type=F32)
           + jnp.dot(vt.astype(BF16), s.astype(BF16), preferred_element_type=F32))
    den = (w_inter * lax.dot_general(n8, qb, nt, preferred_element_type=F32)[0:1]
           + jnp.sum(s, axis=0, keepdims=True))
    o_ref[0, 0, :, blk] = (num / jnp.maximum(jnp.abs(den), jnp.exp(-m_t))).T

    b_end = jnp.sum(f_row, axis=1, keepdims=True)
    g_row = b_end - b_row + i_row
    m_new = jnp.maximum(b_end + m_prev, jnp.max(g_row, axis=1, keepdims=True))
    w_old = jnp.exp(b_end + m_prev - m_new)
    w_tok = jnp.exp(g_row - m_new) * (Dh ** -0.5)
    c_ref[head] = w_old * c + jnp.dot((vt * w_tok).astype(BF16), kb, preferred_element_type=F32)
    w8 = jnp.broadcast_to(w_tok, (8, w_tok.shape[1])).astype(BF16)
    n_ref[head:head + 1, :] = w_old * n_row + jnp.dot(w8, kb, preferred_element_type=F32)[0:1]
    m_ref[head:head + 1, :] = m_new


def _mlstm_kernel(q_ref, k_ref, v_ref, gc_ref, gr_ref, o_ref, c_ref, n_ref, m_ref):
    L = ROW_TILE

    @pl.when(pl.program_id(2) == 0)
    def _():
        c_ref[...] = jnp.zeros_like(c_ref)
        n_ref[...] = jnp.zeros_like(n_ref)
        m_ref[...] = jnp.zeros_like(m_ref)

    row = lax.broadcasted_iota(jnp.int32, (L, L), 0)
    col = lax.broadcasted_iota(jnp.int32, (L, L), 1)
    gc = gc_ref[0]
    gr = gr_ref[0]
    lfc = _log_sigmoid(gc)
    lfr = _log_sigmoid(gr)
    for rev in (False, True):
        @pl.when(pl.program_id(1) == int(rev))
        def _():
            vis_qs = (col >= row) if rev else (col <= row)
            vis_sq = (row >= col) if rev else (row <= col)
            bc = jnp.dot(jnp.where(vis_qs, 1.0, 0.0), lfc, preferred_element_type=F32, precision=HIGHEST)
            br = jnp.dot(lfr, jnp.where(vis_sq, 1.0, 0.0), preferred_element_type=F32, precision=HIGHEST)
            for head in range(ML_HEADS):
                _mlstm_head(rev, head, vis_sq, gc, gr, lfr, bc, br, q_ref, k_ref, v_ref, o_ref,
                            c_ref, n_ref, m_ref)


def _mlstm(p1, gcol, grow):
    b, s, _ = p1.shape
    nch = s // ROW_TILE
    L = ROW_TILE
    W = ML_WIDTH

    def qkv_spec(off):
        return pl.BlockSpec((1, L, W), lambda b, r, c: (b, _scan_chunk(c, nch, r), off))

    return pl.pallas_call(
        _mlstm_kernel,
        out_shape=jax.ShapeDtypeStruct((2, b, s, W), F32),
        grid=(b, 2, nch),
        in_specs=[qkv_spec(0), qkv_spec(1), qkv_spec(2),
                  pl.BlockSpec((1, L, N_GATES), lambda b, r, c: (b, _scan_chunk(c, nch, r), 0)),
                  pl.BlockSpec((1, N_GATES, L), lambda b, r, c: (b, 0, _scan_chunk(c, nch, r)))],
        out_specs=pl.BlockSpec((1, 1, L, W), lambda b, r, c: (r, b, _scan_chunk(c, nch, r), 0)),
        scratch_shapes=[pltpu.VMEM((ML_HEADS, ML_HEAD_DIM, ML_HEAD_DIM), F32),
                        pltpu.VMEM((ML_HEADS, ML_HEAD_DIM), F32),
                        pltpu.VMEM((ML_HEADS, 1), F32)],
        compiler_params=_params(("parallel", "parallel", "arbitrary")),
        name="mlstm",
    )(p1, p1, p1, gcol, grow)


def _rglru_chunk(rev, x_ref, cw_ref, cb_ref, wa_ref, ba_ref, wx_ref, bx_ref, lam_ref, o_ref,
                 ext_ref, a_ref, u_ref, h_ref):
    L = ROW_TILE
    c = pl.program_id(2)
    halo = slice(HALO + L, 2 * HALO + L) if rev else slice(0, HALO)

    @pl.when(c <= 1)
    def _():
        ext_ref[halo, :] = jnp.zeros((HALO, RG_WIDTH), F32)

    @pl.when(c == 0)
    def _():
        h_ref[...] = jnp.zeros_like(h_ref)

    x = x_ref[0]
    ext_ref[HALO:HALO + L, :] = x
    xc = cb_ref[0]
    for j in range(RG_CONV):
        k = RG_CONV - 1 - j
        lo = HALO + k if rev else HALO - k
        xc = xc + cw_ref[0, j:j + 1, :] * ext_ref[lo:lo + L, :]
    ext_ref[halo, :] = x[0:HALO] if rev else x[L - HALO:L]

    sp = _softplus(-lam_ref[0])
    xcb = xc.astype(BF16)
    for n in range(RG_BLOCKS):
        blk = slice(n * RG_BLOCK_DIM, (n + 1) * RG_BLOCK_DIM)
        xb = xcb[:, blk]
        r = _sigmoid(jnp.dot(xb, wa_ref[0, n].astype(BF16), preferred_element_type=F32)
                     + ba_ref[0, :, blk])
        i = _sigmoid(jnp.dot(xb, wx_ref[0, n].astype(BF16), preferred_element_type=F32)
                     + bx_ref[0, :, blk])
        log_a = -RG_C * r * sp[:, blk]
        a = jnp.exp(log_a)
        a_ref[:, blk] = a
        u_ref[:, blk] = jnp.sqrt(1.0 - a * a) * (i * xc[:, blk])

    def step(t, h):
        tt = (L - 1 - t) if rev else t
        h = a_ref[pl.ds(tt, 1), :] * h + u_ref[pl.ds(tt, 1), :]
        o_ref[0, 0, pl.ds(tt, 1), :] = h
        return h

    h_ref[...] = lax.fori_loop(0, L, step, h_ref[...], unroll=8)


def _rglru_kernel(*refs):
    for rev in (False, True):
        @pl.when(pl.program_id(1) == int(rev))
        def _():
            _rglru_chunk(rev, *refs)


def _rglru(p2, cw, cb, wa, ba, wx, bx, lam):
    b, s, _ = p2.shape
    nch = s // ROW_TILE
    L = ROW_TILE
    R = RG_WIDTH
    vec = lambda a: a.reshape(2, 1, R)
    vspec = pl.BlockSpec((1, 1, R), lambda b, r, c: (r, 0, 0))
    wspec = pl.BlockSpec((1, RG_BLOCKS, RG_BLOCK_DIM, RG_BLOCK_DIM), lambda b, r, c: (r, 0, 0, 0))
    return pl.pallas_call(
        _rglru_kernel,
        out_shape=jax.ShapeDtypeStruct((2, b, s, R), F32),
        grid=(b, 2, nch),
        in_specs=[pl.BlockSpec((1, L, R), lambda b, r, c: (b, _scan_chunk(c, nch, r), 0)),
                  pl.BlockSpec((1, RG_CONV, R), lambda b, r, c: (r, 0, 0)),
                  vspec, wspec, vspec, wspec, vspec, vspec],
        out_specs=pl.BlockSpec((1, 1, L, R), lambda b, r, c: (r, b, _scan_chunk(c, nch, r), 0)),
        scratch_shapes=[pltpu.VMEM((L + 2 * HALO, R), F32),
                        pltpu.VMEM((L, R), F32),
                        pltpu.VMEM((L, R), F32),
                        pltpu.VMEM((1, R), F32)],
        compiler_params=_params(("parallel", "parallel", "arbitrary")),
        name="rglru",
    )(p2, cw, vec(cb), wa, vec(ba), wx, vec(bx), vec(lam))


def _gelu_tanh(x):
    return x * (0.5 * (1.0 + jnp.tanh(math.sqrt(2.0 / math.pi) * (x + 0.044715 * (x * x * x)))))


def _merge_kernel(mf_ref, mr_ref, rf_ref, rr_ref, o_ref, ry_ref, g_ref, y_ref):
    ml = mf_ref[0, 0] + mr_ref[0, 0]
    o = o_ref[0]
    g = g_ref[...]
    for h in range(ML_HEADS):
        blk = slice(h * ML_HEAD_DIM, (h + 1) * ML_HEAD_DIM)
        y = _rms(ml[:, blk], g[:, blk]) * _sigmoid(o[:, blk].astype(F32))
        y_ref[0, :, blk] = y.astype(y_ref.dtype)
    rg = rf_ref[0, 0] + rr_ref[0, 0]
    y_ref[0, :, ML_WIDTH:] = (rg * _gelu_tanh(ry_ref[0])).astype(y_ref.dtype)


def _merge(hml, hrg, p1, p2, ml_norm):
    _, b, s, _ = hml.shape
    L = ROW_TILE
    W = ML_WIDTH
    dspec = lambda r: pl.BlockSpec((1, 1, L, W), lambda b, i: (r, b, i, 0))
    return pl.pallas_call(
        _merge_kernel,
        out_shape=jax.ShapeDtypeStruct((b, s, 2 * W), BF16),
        grid=(b, s // L),
        in_specs=[dspec(0), dspec(1), dspec(0), dspec(1),
                  pl.BlockSpec((1, L, W), lambda b, i: (b, i, 3)),
                  pl.BlockSpec((1, L, W), lambda b, i: (b, i, 1)),
                  pl.BlockSpec((1, W), lambda b, i: (0, 0))],
        out_specs=pl.BlockSpec((1, L, 2 * W), lambda b, i: (b, i, 0)),
        compiler_params=_params(("parallel", "parallel")),
        name="merge",
    )(hml, hml, hrg, hrg, p1, p2, ml_norm.reshape(1, W))


def _rope_tables(s_ctx, t):
    rows = t // GRID_W
    row_ids = jnp.repeat(jnp.arange(rows), GRID_W).astype(F32)
    col_ids = jnp.tile(jnp.arange(GRID_W), rows).astype(F32)
    inv = ROPE_THETA ** (-jnp.arange(0, ROPE_AXIS_DIM, 2, dtype=F32) / ROPE_AXIS_DIM)
    ang_r = row_ids[:, None] * inv
    ang_c = col_ids[:, None] * inv
    ang = jnp.concatenate([ang_r, ang_r, ang_c, ang_c], axis=-1)
    cos = jnp.concatenate([jnp.ones((s_ctx, AT_HEAD_DIM), F32), jnp.cos(ang)], axis=0)
    sin = jnp.concatenate([jnp.zeros((s_ctx, AT_HEAD_DIM), F32), jnp.sin(ang)], axis=0)
    return cos, sin


def _rope(x, cos, sin):
    half = ROPE_AXIS_DIM // 2
    lane = lax.broadcasted_iota(jnp.int32, x.shape, 1)
    first = (lane % ROPE_AXIS_DIM) < half
    rot = jnp.where(first, -pltpu.roll(x, AT_HEAD_DIM - half, 1), pltpu.roll(x, half, 1))
    return x * cos + rot * sin


def _qkprep_kernel(p_ref, qn_ref, kn_ref, cos_ref, sin_ref, q_ref, k_ref, v_ref):
    cos = cos_ref[...]
    sin = sin_ref[...]
    Dh = AT_HEAD_DIM
    for h in range(AT_HEADS):
        x = _rope(_rms(p_ref[0, :, h * Dh:(h + 1) * Dh], qn_ref[...]), cos, sin)
        q_ref[0, :, h * Dh:(h + 1) * Dh] = (x * (Dh ** -0.5)).astype(q_ref.dtype)
    for h in range(AT_KV_HEADS):
        off = (AT_HEADS + h) * Dh
        x = _rope(_rms(p_ref[0, :, off:off + Dh], kn_ref[...]), cos, sin)
        k_ref[0, :, h * Dh:(h + 1) * Dh] = x.astype(k_ref.dtype)
    for h in range(AT_KV_HEADS):
        off = (AT_HEADS + AT_KV_HEADS + h) * Dh
        v_ref[0, :, 2 * h * Dh:(2 * h + 1) * Dh] = p_ref[0, :, off:off + Dh].astype(v_ref.dtype)
        v_ref[0, :, (2 * h + 1) * Dh:(2 * h + 2) * Dh] = jnp.ones((p_ref.shape[1], Dh), v_ref.dtype)


def _qkprep(p, q_norm, k_norm, cos, sin):
    b, s, w = p.shape
    L = ROW_TILE
    Dh = AT_HEAD_DIM
    qd, kd = AT_HEADS * Dh, AT_KV_HEADS * Dh
    return pl.pallas_call(
        _qkprep_kernel,
        out_shape=(jax.ShapeDtypeStruct((b, s, qd), BF16),
                   jax.ShapeDtypeStruct((b, s, kd), BF16),
                   jax.ShapeDtypeStruct((b, s, 2 * kd), BF16)),
        grid=(b, s // L),
        in_specs=[pl.BlockSpec((1, L, w), lambda b, i: (b, i, 0)),
                  pl.BlockSpec((1, Dh), lambda b, i: (0, 0)),
                  pl.BlockSpec((1, Dh), lambda b, i: (0, 0)),
                  pl.BlockSpec((L, Dh), lambda b, i: (i, 0)),
                  pl.BlockSpec((L, Dh), lambda b, i: (i, 0))],
        out_specs=(pl.BlockSpec((1, L, qd), lambda b, i: (b, i, 0)),
                   pl.BlockSpec((1, L, kd), lambda b, i: (b, i, 0)),
                   pl.BlockSpec((1, L, 2 * kd), lambda b, i: (b, i, 0))),
        compiler_params=_params(("parallel", "parallel")),
        name="qkprep",
    )(p, q_norm.reshape(1, Dh), k_norm.reshape(1, Dh), cos, sin)


def _attend(q_ref, k, v, o_ref):
    Dh = AT_HEAD_DIM
    for g in range(AT_GROUP):
        q = q_ref[0, :, g * Dh:(g + 1) * Dh]
        s = lax.dot_general(q, k, (((1,), (1,)), ((), ())), preferred_element_type=F32)
        e = jnp.exp((s - jnp.max(s, axis=-1, keepdims=True)).astype(BF16))
        o = jnp.dot(e, v, preferred_element_type=F32)
        o_ref[0, :, g * Dh:(g + 1) * Dh] = (o[:, :Dh] / o[:, Dh:]).astype(o_ref.dtype)


def _attn_kernel(q_ref, k_ref, v_ref, o_ref):
    i = pl.program_id(2)

    @pl.when(i == 0)
    def _():
        _attend(q_ref, k_ref[0, 0:ROW_TILE, :], v_ref[0, 0:ROW_TILE, :], o_ref)

    @pl.when(i > 0)
    def _():
        _attend(q_ref, k_ref[0], v_ref[0], o_ref)


def _attention(q, k, v):
    b, s, qd = q.shape
    L = ROW_TILE
    Dh = AT_HEAD_DIM
    gw = AT_GROUP * Dh
    return pl.pallas_call(
        _attn_kernel,
        out_shape=jax.ShapeDtypeStruct((b, s, qd), BF16),
        grid=(b, AT_KV_HEADS, s // L),
        in_specs=[pl.BlockSpec((1, L, gw), lambda b, h, i: (b, i, h)),
                  pl.BlockSpec((1, s, Dh), lambda b, h, i: (b, 0, h)),
                  pl.BlockSpec((1, s, 2 * Dh), lambda b, h, i: (b, 0, h))],
        out_specs=pl.BlockSpec((1, L, gw), lambda b, h, i: (b, i, h)),
        compiler_params=_params(("parallel", "parallel", "parallel")),
        name="attention",
    )(q, k, v)


def _cast_kernel(w_ref, o_ref):
    o_ref[...] = w_ref[...].astype(o_ref.dtype)


def _cast_bf16(w, layer):
    _, k, n = w.shape
    tk = math.gcd(k, 512)
    return pl.pallas_call(
        _cast_kernel,
        out_shape=jax.ShapeDtypeStruct((k, n), BF16),
        grid=(k // tk,),
        in_specs=[pl.BlockSpec((None, tk, n), lambda i: (layer, i, 0))],
        out_specs=pl.BlockSpec((tk, n), lambda i: (i, 0)),
        compiler_params=_params(("parallel",)),
        name="cast_bf16",
    )(w)


def _outproj_kernel(a_ref, w_ref, x_ref, g_ref, m_ref, wr_ref, xo_ref, h_ref, lg_ref):
    m = m_ref[0, 0]
    y = jnp.dot(a_ref[0], w_ref[...], preferred_element_type=F32)
    x = x_ref[0] + m[2:3] * y
    xo_ref[0] = x
    h = _rms(x, g_ref[...]) * (1.0 + m[4:5]) + m[3:4]
    hb = h.astype(h_ref.dtype)
    h_ref[0] = hb
    w2 = wr_ref[...]
    h_lo = (h - hb.astype(F32)).astype(BF16)
    both = jnp.dot(hb, w2, preferred_element_type=F32)
    lg_ref[0] = (both[:, :LANES] + both[:, LANES:]
                 + jnp.dot(h_lo, w2[:, :LANES], preferred_element_type=F32))


def _split_router(w_router):
    w = jnp.pad(w_router, ((0, 0), (0, 0), (0, LANES - w_router.shape[2])))
    hi = w.astype(BF16)
    lo = (w - hi.astype(F32)).astype(BF16)
    return jnp.concatenate([hi, lo], axis=2)


def _outproj(a, w, xs, g, mt, wr):
    b, s, d = xs.shape
    L = ROW_TILE
    tile = pl.BlockSpec((1, L, d), lambda b, i: (b, i, 0))
    return pl.pallas_call(
        _outproj_kernel,
        out_shape=(jax.ShapeDtypeStruct((b, s, d), F32),
                   jax.ShapeDtypeStruct((b, s, d), BF16),
                   jax.ShapeDtypeStruct((b, s, LANES), F32)),
        grid=(b, s // L),
        in_specs=[tile, pl.BlockSpec((d, d), lambda b, i: (0, 0)), tile,
                  pl.BlockSpec((1, d), lambda b, i: (0, 0)), _mod_spec(d),
                  pl.BlockSpec((d, 2 * LANES), lambda b, i: (0, 0))],
        out_specs=(tile, tile, pl.BlockSpec((1, L, LANES), lambda b, i: (b, i, 0))),
        compiler_params=_params(("parallel", "parallel")),
        name="outproj",
    )(a, w, xs, g.reshape(1, d), mt, wr)


def _excl_prefix(mask):
    e, n = mask.shape
    groups = n // LANES
    r = lax.broadcasted_iota(jnp.int32, (LANES, LANES), 0)
    c = lax.broadcasted_iota(jnp.int32, (LANES, LANES), 1)
    upper = jnp.where(r < c, 1.0, 0.0).astype(BF16)
    stk = jnp.concatenate([mask[:, g * LANES:(g + 1) * LANES] for g in range(groups)], axis=0)
    within = jnp.dot(stk.astype(BF16), upper, preferred_element_type=F32)
    tot = jnp.sum(stk, axis=1, keepdims=True)
    outs = []
    off = jnp.zeros((e, 1), F32)
    for g in range(groups):
        outs.append(within[g * e:(g + 1) * e] + off)
        off = off + tot[g * e:(g + 1) * e]
    return jnp.concatenate(outs, axis=1)


def _top_slots(aff, cap):
    v = lax.bitcast_convert_type(aff, jnp.int32)
    thr = jnp.zeros((aff.shape[0], 1), jnp.int32)
    for bit in range(30, -1, -1):
        cand = thr | (1 << bit)
        cnt = jnp.sum(jnp.where(v >= cand, 1.0, 0.0), axis=1, keepdims=True)
        thr = jnp.where(cnt >= cap, cand, thr)
    gt = jnp.where(v > thr, 1.0, 0.0)
    eq = jnp.where(v == thr, 1.0, 0.0)
    need = cap - jnp.sum(gt, axis=1, keepdims=True)
    sel = gt + eq * jnp.where(_excl_prefix(eq) < need, 1.0, 0.0)
    return jnp.where(sel > 0.0, _excl_prefix(sel), -1.0)


def _route_kernel(lg_ref, pos_ref, aff_ref, post_ref):
    s = lg_ref.shape[1]
    E = N_EXPERTS
    lt = lg_ref[0].T[0:E, :]
    ex = jnp.exp(lt - jnp.max(lt, axis=0, keepdims=True))
    aff = ex / jnp.sum(ex, axis=0, keepdims=True)
    aff_ref[0] = aff
    pos = jnp.concatenate(
        [_top_slots(aff[:, :ROW_TILE], EC_FACTOR * ROW_TILE // E),
         _top_slots(aff[:, ROW_TILE:], EC_FACTOR * (s - ROW_TILE) // E)], axis=1)
    pos_ref[0] = pos
    padded = jnp.concatenate([pos, jnp.full((LANES - E, s), -1.0, F32)], axis=0)
    post_ref[0] = padded.T


def _route(logits):
    b, s, _ = logits.shape
    E = N_EXPERTS
    return pl.pallas_call(
        _route_kernel,
        out_shape=(jax.ShapeDtypeStruct((b, E, s), F32),
                   jax.ShapeDtypeStruct((b, E, s), F32),
                   jax.ShapeDtypeStruct((b, s, LANES), F32)),
        grid=(b,),
        in_specs=[pl.BlockSpec((1, s, LANES), lambda b: (b, 0, 0))],
        out_specs=(pl.BlockSpec((1, E, s), lambda b: (b, 0, 0)),
                   pl.BlockSpec((1, E, s), lambda b: (b, 0, 0)),
                   pl.BlockSpec((1, s, LANES), lambda b: (b, 0, 0))),
        compiler_params=_params(("parallel",)),
        name="route",
    )(logits)


def _gather_kernel(h_ref, pos_ref, aff_ref, xx_ref, xc_ref, gx_ref, gc_ref):
    s = h_ref.shape[1]
    group = xx_ref.shape[0]
    for lo, n, x_ref, g_ref in ((0, ROW_TILE, xc_ref, gc_ref), (ROW_TILE, s - ROW_TILE, xx_ref, gx_ref)):
        cap = x_ref.shape[1]
        slot = lax.broadcasted_iota(jnp.int32, (cap, n), 0).astype(F32)
        hits = [pos_ref[0, j, :, lo:lo + n] == slot for j in range(group)]
        onehot = jnp.concatenate([jnp.where(hit, 1.0, 0.0).astype(BF16) for hit in hits], axis=0)
        rows = jnp.dot(onehot, h_ref[0, lo:lo + n, :], preferred_element_type=F32)
        for j in range(group):
            x_ref[j] = rows[j * cap:(j + 1) * cap].astype(x_ref.dtype)
            g_ref[j] = jnp.sum(jnp.where(hits[j], aff_ref[0, j, :, lo:lo + n], 0.0), axis=1, keepdims=True)


def _gather(h, pos, aff):
    b, s, d = h.shape
    E = N_EXPERTS
    G = GATHER_GROUP
    cap_c = EC_FACTOR * ROW_TILE // E
    cap_x = EC_FACTOR * (s - ROW_TILE) // E
    row = pl.BlockSpec((1, G, 1, s), lambda b, e: (b, e, 0, 0))
    return pl.pallas_call(
        _gather_kernel,
        out_shape=(jax.ShapeDtypeStruct((E, b * cap_x, d), BF16),
                   jax.ShapeDtypeStruct((E, b * cap_c, d), BF16),
                   jax.ShapeDtypeStruct((E, b * cap_x, 1), F32),
                   jax.ShapeDtypeStruct((E, b * cap_c, 1), F32)),
        grid=(b, E // G),
        in_specs=[pl.BlockSpec((1, s, d), lambda b, e: (b, 0, 0)), row, row],
        out_specs=(pl.BlockSpec((G, cap_x, d), lambda b, e: (e, b, 0)),
                   pl.BlockSpec((G, cap_c, d), lambda b, e: (e, b, 0)),
                   pl.BlockSpec((G, cap_x, 1), lambda b, e: (e, b, 0)),
                   pl.BlockSpec((G, cap_c, 1), lambda b, e: (e, b, 0))),
        compiler_params=_params(("parallel", "arbitrary")),
        name="gather",
    )(h, pos.reshape(b, E, 1, s), aff.reshape(b, E, 1, s))


def _ffn_kernel(xx_ref, xc_ref, gx_ref, gc_ref, wg_ref, wu_ref, wd_ref, yx_ref, yc_ref, ax_ref, ac_ref):
    step = pl.program_id(1)
    nf = ax_ref.shape[0]
    pairs = ((xx_ref, ax_ref, gx_ref, yx_ref), (xc_ref, ac_ref, gc_ref, yc_ref))

    @pl.when(step < nf)
    def _():
        wg = wg_ref[0].astype(BF16)
        wu = wu_ref[0].astype(BF16)
        for x_ref, a_ref, _, _ in pairs:
            x = x_ref[0]
            g = jnp.dot(x, wg, preferred_element_type=F32)
            u = jnp.dot(x, wu, preferred_element_type=F32)
            a_ref[step] = (g * _sigmoid(g) * u).astype(BF16)

    @pl.when(step >= nf)
    def _():
        wd = wd_ref[0].astype(BF16)
        tf = ax_ref.shape[2]
        for _, a_ref, g_ref, y_ref in pairs:
            y = jnp.dot(a_ref[0], wd[0:tf], preferred_element_type=F32)
            for f in range(1, nf):
                y = y + jnp.dot(a_ref[f], wd[f * tf:(f + 1) * tf], preferred_element_type=F32)
            y_ref[0] = (y * g_ref[0]).astype(y_ref.dtype)


def _ffn(xx, xc, gx, gc, w_gate, w_up, w_down, layer):
    E, rx, d = xx.shape
    rc = xc.shape[1]
    ff = w_gate.shape[3]
    nf = ff // FFN_HIDDEN_TILE
    nd = d // FFN_OUT_TILE
    xspec = lambda r, w: pl.BlockSpec((1, r, w), lambda e, s: (e, 0, 0))
    hid = lambda e, s: (layer, e, 0, jnp.minimum(s, nf - 1))
    out_tile = lambda s: jnp.maximum(s - nf, 0)
    yspec = lambda r: pl.BlockSpec((1, r, FFN_OUT_TILE), lambda e, s: (e, 0, out_tile(s)))
    return pl.pallas_call(
        _ffn_kernel,
        out_shape=(jax.ShapeDtypeStruct((E, rx, d), BF16), jax.ShapeDtypeStruct((E, rc, d), BF16)),
        grid=(E, nf + nd),
        in_specs=[xspec(rx, d), xspec(rc, d), xspec(rx, 1), xspec(rc, 1),
                  pl.BlockSpec((None, 1, d, FFN_HIDDEN_TILE), hid),
                  pl.BlockSpec((None, 1, d, FFN_HIDDEN_TILE), hid),
                  pl.BlockSpec((None, 1, ff, FFN_OUT_TILE), lambda e, s: (layer, e, 0, out_tile(s)))],
        out_specs=(yspec(rx), yspec(rc)),
        scratch_shapes=[pltpu.VMEM((nf, rx, FFN_HIDDEN_TILE), BF16), pltpu.VMEM((nf, rc, FFN_HIDDEN_TILE), BF16)],
        compiler_params=_params(("parallel", "arbitrary")),
        name="expert_ffn",
    )(xx, xc, gx, gc, w_gate, w_up, w_down)


def _combine(pt, y_ref):
    cap = y_ref.shape[1]
    n = pt.shape[0]
    slot = lax.broadcasted_iota(jnp.int32, (n, cap), 1).astype(F32)
    acc = jnp.zeros((n, y_ref.shape[2]), F32)
    for e in range(N_EXPERTS):
        onehot = jnp.where(pt[:, e:e + 1] == slot, 1.0, 0.0).astype(BF16)
        acc = acc + jnp.dot(onehot, y_ref[e], preferred_element_type=F32)
    return acc


def _scatter_kernel(x_ref, pt_ref, yx_ref, yc_ref, m_ref, gn_ref, mn_ref, o_ref, h_ref):
    i = pl.program_id(1)
    gate = m_ref[0, 0][5:6]
    mn = mn_ref[0, 0]

    def finish(acc):
        x = x_ref[0] + gate * acc
        o_ref[0] = x
        h_ref[0] = (_rms(x, gn_ref[...]) * (1.0 + mn[1:2]) + mn[0:1]).astype(h_ref.dtype)

    @pl.when(i == 0)
    def _():
        finish(_combine(pt_ref[0], yc_ref))

    @pl.when(i > 0)
    def _():
        finish(_combine(pt_ref[0], yx_ref))


def _scatter(xs, post, yx, yc, mt, g_next, mt_next):
    b, s, d = xs.shape
    E = N_EXPERTS
    L = ROW_TILE
    cap_x = yx.shape[1] // b
    cap_c = yc.shape[1] // b
    tile = pl.BlockSpec((1, L, d), lambda b, i: (b, i, 0))
    return pl.pallas_call(
        _scatter_kernel,
        out_shape=(jax.ShapeDtypeStruct((b, s, d), F32), jax.ShapeDtypeStruct((b, s, d), BF16)),
        grid=(b, s // L),
        in_specs=[tile,
                  pl.BlockSpec((1, L, LANES), lambda b, i: (b, i, 0)),
                  pl.BlockSpec((E, cap_x, d), lambda b, i: (0, b, 0)),
                  pl.BlockSpec((E, cap_c, d), lambda b, i: (0, b, 0)),
                  _mod_spec(d),
                  pl.BlockSpec((1, d), lambda b, i: (0, 0)),
                  _mod_spec(d)],
        out_specs=(tile, tile),
        compiler_params=_params(("parallel", "arbitrary")),
        name="scatter",
    )(xs, post, yx, yc, mt, g_next.reshape(1, d), mt_next)


def _scatter_last_kernel(x_ref, pt_ref, yx_ref, m_ref, o_ref):
    o_ref[0] = x_ref[0] + m_ref[0, 0][5:6] * _combine(pt_ref[0], yx_ref)


def _scatter_last(xs, post, yx, mt):
    b, s, d = xs.shape
    E = N_EXPERTS
    L = ROW_TILE
    cap_x = yx.shape[1] // b
    return pl.pallas_call(
        _scatter_last_kernel,
        out_shape=jax.ShapeDtypeStruct((b, s - L, d), F32),
        grid=(b, s // L - 1),
        in_specs=[pl.BlockSpec((1, L, d), lambda b, i: (b, i + 1, 0)),
                  pl.BlockSpec((1, L, LANES), lambda b, i: (b, i + 1, 0)),
                  pl.BlockSpec((E, cap_x, d), lambda b, i: (0, b, 0)),
                  pl.BlockSpec((1, 1, 6, d), lambda b, i: (b, 1, 0, 0))],
        out_specs=pl.BlockSpec((1, L, d), lambda b, i: (b, i, 0)),
        compiler_params=_params(("parallel", "arbitrary")),
        name="scatter_last",
    )(xs, post, yx, mt)


def kernel(x, c, ctx, c_ctx, w_mod, b_mod, norm_mix, norm_ffn, ab_w_in, ab_gate_b, ml_norm,
           rg_conv_w, rg_conv_b, rg_wa, rg_ba, rg_wx, rg_bx, rg_lam, ab_w_out,
           at_w_qkv, at_q_norm, at_k_norm, at_w_o,
           moe_w_router, moe_w_gate, moe_w_up, moe_w_down):
    B, T, D = x.shape
    n_ctx = ctx.shape[1]
    assert n_ctx == ROW_TILE and T % ROW_TILE == 0 and B <= 7
    S = n_ctx + T
    depth = w_mod.shape[0]
    E = moe_w_router.shape[2]

    c8 = jnp.concatenate([c, c_ctx[None], jnp.zeros((7 - B, D), F32)], axis=0)
    modv = _mods(c8, w_mod, b_mod).reshape(depth, 8, 6, D)
    mod_c = jnp.broadcast_to(modv[:, B][:, None], (depth, B, 6, D))
    mtab = jnp.stack([mod_c, modv[:, :B]], axis=2)
    cos, sin = _rope_tables(n_ctx, T)
    wr2 = _split_router(moe_w_router)
    w_in_t = jnp.swapaxes(ab_w_in, 1, 2)

    xs, h = _norm_mod(ctx, x, norm_mix[0], mtab[0])
    for l in range(depth):
        j = l // 2
        mt = mtab[l]
        h = h.reshape(B * S, D)
        if l % 2 == 0:
            g0 = 4 * ML_WIDTH
            p1 = _matmul_t(h, w_in_t, j, 0, g0, tn=512, out_dtype=BF16, name="ab_in_qkvo").reshape(B, S, g0)
            p2 = _matmul_t(h, w_in_t, j, g0 + N_GATES, 2 * RG_WIDTH, tn=512,
                           name="ab_in_rg").reshape(B, S, 2 * RG_WIDTH)
            gates = _matmul_t(h, w_in_t, j, g0, N_GATES, tn=N_GATES, name="ab_in_gates").reshape(B, S, N_GATES)
            gcol = gates + ab_gate_b[j].reshape(-1)
            hml = _mlstm(p1, gcol, jnp.swapaxes(gcol, 1, 2))
            hrg = _rglru(p2, rg_conv_w[j], rg_conv_b[j], rg_wa[j], rg_ba[j], rg_wx[j], rg_bx[j], rg_lam[j])
            mixed = _merge(hml, hrg, p1, p2, ml_norm[j])
            w_out = _cast_bf16(ab_w_out, j)
        else:
            wq = at_w_qkv.shape[2]
            p = _matmul(h, at_w_qkv, wq, tn=512, layer=j, name="at_qkv").reshape(B, S, wq)
            q, k, v = _qkprep(p, at_q_norm[j], at_k_norm[j], cos, sin)
            mixed = _attention(q, k, v)
            w_out = _cast_bf16(at_w_o, j)
        xs, h2, logits = _outproj(mixed, w_out, xs, norm_ffn[l], mt, wr2[l])
        pos, aff, post = _route(logits)
        xx, xc, gx, gc = _gather(h2, pos, aff)
        yx, yc = _ffn(xx, xc, gx, gc, moe_w_gate, moe_w_up, moe_w_down, l)
        if l + 1 == depth:
            return _scatter_last(xs, post, yx, mt)
        xs, h = _scatter(xs, post, yx, yc, mt, norm_mix[l + 1], mtab[l + 1])
```

```python
import functools
import math

import jax
import jax.numpy as jnp
from jax import lax
from jax.experimental import pallas as pl
from jax.experimental.pallas import tpu as pltpu

F32 = jnp.float32
BF16 = jnp.bfloat16
HIGHEST = lax.Precision.HIGHEST

NORM_EPS = 1e-6
GRID_W = 64
ROW_TILE = 256
LANES = 128
ML_HEADS = 8
ML_HEAD_DIM = 128
ML_WIDTH = ML_HEADS * ML_HEAD_DIM
N_GATES = 4 * ML_HEADS
RG_WIDTH = 1024
RG_BLOCKS = 8
RG_BLOCK_DIM = RG_WIDTH // RG_BLOCKS
RG_CONV = 4
RG_C = 8.0
AT_HEAD_DIM = 128
AT_HEADS = 16
AT_KV_HEADS = 4
AT_GROUP = AT_HEADS // AT_KV_HEADS
ROPE_AXIS_DIM = AT_HEAD_DIM // 2
ROPE_THETA = 10000.0
N_EXPERTS = 16
EC_FACTOR = 2
MM_ROWS = 2304
FFN_HIDDEN_TILE = 256
FFN_OUT_TILE = 512
GATHER_GROUP = 4
HALO = 8
VMEM_LIMIT = 56 * 1024 * 1024


def _params(sem, vmem=VMEM_LIMIT):
    return pltpu.CompilerParams(dimension_semantics=sem, vmem_limit_bytes=vmem)


def _sigmoid(x):
    return 0.5 * jnp.tanh(0.5 * x) + 0.5


def _softplus(x):
    return jnp.maximum(x, 0.0) + jnp.log1p(jnp.exp(-jnp.abs(x)))


def _log_sigmoid(x):
    return -_softplus(-x)


def _mods_kernel(c_ref, w_ref, b_ref, o_ref):
    c = c_ref[...]
    a = c * _sigmoid(c)
    o_ref[0] = jnp.dot(a.astype(BF16), w_ref[0].astype(BF16), preferred_element_type=F32) + b_ref[0]


def _mods(c8, w_mod, b_mod):
    depth, d, n = w_mod.shape
    tn = 1024
    return pl.pallas_call(
        _mods_kernel,
        out_shape=jax.ShapeDtypeStruct((depth, 8, n), F32),
        grid=(depth, n // tn),
        in_specs=[pl.BlockSpec((8, d), lambda l, j: (0, 0)),
                  pl.BlockSpec((1, d, tn), lambda l, j: (l, 0, j)),
                  pl.BlockSpec((1, 1, tn), lambda l, j: (l, 0, j))],
        out_specs=pl.BlockSpec((1, 8, tn), lambda l, j: (l, 0, j)),
        compiler_params=_params(("parallel", "parallel")),
        name="mods",
    )(c8, w_mod, b_mod.reshape(depth, 1, n))


def _mod_spec(d):
    return pl.BlockSpec((1, 1, 6, d), lambda b, i: (b, jnp.minimum(i, 1), 0, 0))


def _rms(x, g):
    return x * lax.rsqrt(jnp.mean(x * x, axis=-1, keepdims=True) + NORM_EPS) * g


def _norm_kernel(c_ref, x_ref, g_ref, m_ref, xs_ref, h_ref):
    m = m_ref[0, 0]

    def emit(x):
        xs_ref[0] = x
        h_ref[0] = (_rms(x, g_ref[...]) * (1.0 + m[1:2]) + m[0:1]).astype(h_ref.dtype)

    @pl.when(pl.program_id(1) == 0)
    def _():
        emit(c_ref[0])

    @pl.when(pl.program_id(1) > 0)
    def _():
        emit(x_ref[0])


def _norm_mod(ctx, x, g, mt):
    b, t, d = x.shape
    L = ROW_TILE
    s = L + t
    tile = pl.BlockSpec((1, L, d), lambda b, i: (b, i, 0))
    return pl.pallas_call(
        _norm_kernel,
        out_shape=(jax.ShapeDtypeStruct((b, s, d), F32), jax.ShapeDtypeStruct((b, s, d), BF16)),
        grid=(b, s // L),
        in_specs=[pl.BlockSpec((1, L, d), lambda b, i: (b, 0, 0)),
                  pl.BlockSpec((1, L, d), lambda b, i: (b, jnp.maximum(i - 1, 0), 0)),
                  pl.BlockSpec((1, d), lambda b, i: (0, 0)),
                  _mod_spec(d)],
        out_specs=(tile, tile),
        compiler_params=_params(("parallel", "arbitrary")),
        name="norm_mod",
    )(ctx, x, g.reshape(1, d), mt)


def _mm_kernel(a_ref, w_ref, o_ref):
    o_ref[...] = jnp.dot(a_ref[...], w_ref[...].astype(BF16),
                         preferred_element_type=F32).astype(o_ref.dtype)


def _matmul(a, w, n, *, tn, layer=None, out_dtype=F32, name="matmul"):
    m, k = a.shape
    tm = math.gcd(m, MM_ROWS)
    if layer is None:
        wspec = pl.BlockSpec((k, tn), lambda i, j: (0, j))
    else:
        wspec = pl.BlockSpec((None, k, tn), lambda i, j: (layer, 0, j))
    return pl.pallas_call(
        _mm_kernel,
        out_shape=jax.ShapeDtypeStruct((m, n), out_dtype),
        grid=(m // tm, n // tn),
        in_specs=[pl.BlockSpec((tm, k), lambda i, j: (i, 0)), wspec],
        out_specs=pl.BlockSpec((tm, tn), lambda i, j: (i, j)),
        compiler_params=_params(("parallel", "parallel")),
        name=name,
    )(a, w)


def _mm_t_kernel(a_ref, w_ref, o_ref):
    o_ref[...] = lax.dot_general(a_ref[...], w_ref[0].astype(BF16), (((1,), (1,)), ((), ())),
                                 preferred_element_type=F32).astype(o_ref.dtype)


def _matmul_t(a, w_t, layer, row0, n, *, tn, out_dtype=F32, name="matmul_t"):
    m, k = a.shape
    tm = math.gcd(m, MM_ROWS)
    assert row0 % 8 == 0 and n % tn == 0
    wspec = pl.BlockSpec((pl.Element(1), pl.Element(tn), pl.Element(k)),
                         lambda i, j: (layer, pl.multiple_of(row0 + tn * j, 8), 0))
    return pl.pallas_call(
        _mm_t_kernel,
        out_shape=jax.ShapeDtypeStruct((m, n), out_dtype),
        grid=(m // tm, n // tn),
        in_specs=[pl.BlockSpec((tm, k), lambda i, j: (i, 0)), wspec],
        out_specs=pl.BlockSpec((tm, tn), lambda i, j: (i, j)),
        compiler_params=_params(("parallel", "parallel")),
        name=name,
    )(a, w_t)


def _scan_chunk(c, n_chunks, rev):
    return jnp.where(jnp.logical_or(c == 0, rev == 0), c, n_chunks - c)


def _mlstm_head(rev, head, vis, gc, gr, lfr, bc, br, q_ref, k_ref, v_ref, o_ref, c_ref, n_ref, m_ref):
    Dh = ML_HEAD_DIM
    blk = slice(head * Dh, (head + 1) * Dh)
    gi = int(rev) * 2 * ML_HEADS + head
    gf = gi + ML_HEADS
    nt = (((1,), (1,)), ((), ()))
    i_row = gr[gi:gi + 1, :]
    f_row = lfr[gf:gf + 1, :]
    b_row = br[gf:gf + 1, :]
    src = gc[:, gi:gi + 1] - bc[:, gf:gf + 1]
    d = jnp.where(vis, b_row + src, -jnp.inf)
    m_prev = m_ref[head:head + 1, :]
    inter = b_row + m_prev
    m_t = jnp.maximum(inter, jnp.max(d, axis=0, keepdims=True))
    w_inter = jnp.exp(inter - m_t)

    qb = q_ref[0, :, blk]
    kb = k_ref[0, :, blk]
    vt = v_ref[0, :, blk].astype(F32).T
    c = c_ref[head]
    n_row = n_ref[head:head + 1, :]
    n8 = jnp.broadcast_to(n_row, (8, Dh)).astype(BF16)
    log_scale = -0.5 * math.log(Dh)
    s = lax.dot_general(kb, qb, nt, preferred_element_type=F32) * jnp.exp(d - (m_t - log_scale))
    num = (w_inter * lax.dot_general(c.astype(BF16), qb, nt, preferred_element_type=F32)
           + jnp.dot(vt.astype(BF16), s.astype(BF16), preferred_element_type=F32))
    den = (w_inter * lax.dot_general(n8, qb, nt, preferred_element_type=F32)[0:1]
           + jnp.sum(s, axis=0, keepdims=True))
    o_ref[0, 0, :, blk] = (num / jnp.maximum(jnp.abs(den), jnp.exp(-m_t))).T

    b_end = jnp.sum(f_row, axis=1, keepdims=True)
    g_row = b_end - b_row + i_row
    m_new = jnp.maximum(b_end + m_prev, jnp.max(g_row, axis=1, keepdims=True))
    w_old = jnp.exp(b_end + m_prev - m_new)
    w_tok = jnp.exp(g_row - m_new) * (Dh ** -0.5)
    c_ref[head] = w_old * c + jnp.dot((vt * w_tok).astype(BF16), kb, preferred_element_type=F32)
    w8 = jnp.broadcast_to(w_tok, (8, w_tok.shape[1])).astype(BF16)
    n_ref[head:head + 1, :] = w_old * n_row + jnp.dot(w8, kb, preferred_element_type=F32)[0:1]
    m_ref[head:head + 1, :] = m_new


def _mlstm_kernel(q_ref, k_ref, v_ref, gc_ref, gr_ref, o_ref, c_ref, n_ref, m_ref):
    L = ROW_TILE

    @pl.when(pl.program_id(2) == 0)
    def _():
        c_ref[...] = jnp.zeros_like(c_ref)
        n_ref[...] = jnp.zeros_like(n_ref)
        m_ref[...] = jnp.zeros_like(m_ref)

    row = lax.broadcasted_iota(jnp.int32, (L, L), 0)
    col = lax.broadcasted_iota(jnp.int32, (L, L), 1)
    gc = gc_ref[0]
    gr = gr_ref[0]
    lfc = _log_sigmoid(gc)
    lfr = _log_sigmoid(gr)
    for rev in (False, True):
        @pl.when(pl.program_id(1) == int(rev))
        def _():
            vis_qs = (col >= row) if rev else (col <= row)
            vis_sq = (row >= col) if rev else (row <= col)
            bc = jnp.dot(jnp.where(vis_qs, 1.0, 0.0), lfc, preferred_element_type=F32, precision=HIGHEST)
            br = jnp.dot(lfr, jnp.where(vis_sq, 1.0, 0.0), preferred_element_type=F32, precision=HIGHEST)
            for head in range(ML_HEADS):
                _mlstm_head(rev, head, vis_sq, gc, gr, lfr, bc, br, q_ref, k_ref, v_ref, o_ref,
                            c_ref, n_ref, m_ref)


def _mlstm(p1, gcol, grow):
    b, s, _ = p1.shape
    nch = s // ROW_TILE
    L = ROW_TILE
    W = ML_WIDTH

    def qkv_spec(off):
        return pl.BlockSpec((1, L, W), lambda b, r, c: (b, _scan_chunk(c, nch, r), off))

    return pl.pallas_call(
        _mlstm_kernel,
        out_shape=jax.ShapeDtypeStruct((2, b, s, W), F32),
        grid=(b, 2, nch),
        in_specs=[qkv_spec(0), qkv_spec(1), qkv_spec(2),
                  pl.BlockSpec((1, L, N_GATES), lambda b, r, c: (b, _scan_chunk(c, nch, r), 0)),
                  pl.BlockSpec((1, N_GATES, L), lambda b, r, c: (b, 0, _scan_chunk(c, nch, r)))],
        out_specs=pl.BlockSpec((1, 1, L, W), lambda b, r, c: (r, b, _scan_chunk(c, nch, r), 0)),
        scratch_shapes=[pltpu.VMEM((ML_HEADS, ML_HEAD_DIM, ML_HEAD_DIM), F32),
                        pltpu.VMEM((ML_HEADS, ML_HEAD_DIM), F32),
                        pltpu.VMEM((ML_HEADS, 1), F32)],
        compiler_params=_params(("parallel", "parallel", "arbitrary")),
        name="mlstm",
    )(p1, p1, p1, gcol, grow)


def _rglru_chunk(rev, x_ref, cw_ref, cb_ref, wa_ref, ba_ref, wx_ref, bx_ref, lam_ref, o_ref,
                 ext_ref, a_ref, u_ref, h_ref):
    L = ROW_TILE
    c = pl.program_id(2)
    halo = slice(HALO + L, 2 * HALO + L) if rev else slice(0, HALO)

    @pl.when(c <= 1)
    def _():
        ext_ref[halo, :] = jnp.zeros((HALO, RG_WIDTH), F32)

    @pl.when(c == 0)
    def _():
        h_ref[...] = jnp.zeros_like(h_ref)

    x = x_ref[0]
    ext_ref[HALO:HALO + L, :] = x
    xc = cb_ref[0]
    for j in range(RG_CONV):
        k = RG_CONV - 1 - j
        lo = HALO + k if rev else HALO - k
        xc = xc + cw_ref[0, j:j + 1, :] * ext_ref[lo:lo + L, :]
    ext_ref[halo, :] = x[0:HALO] if rev else x[L - HALO:L]

    sp = _softplus(-lam_ref[0])
    xcb = xc.astype(BF16)
    for n in range(RG_BLOCKS):
        blk = slice(n * RG_BLOCK_DIM, (n + 1) * RG_BLOCK_DIM)
        xb = xcb[:, blk]
        r = _sigmoid(jnp.dot(xb, wa_ref[0, n].astype(BF16), preferred_element_type=F32)
                     + ba_ref[0, :, blk])
        i = _sigmoid(jnp.dot(xb, wx_ref[0, n].astype(BF16), preferred_element_type=F32)
                     + bx_ref[0, :, blk])
        log_a = -RG_C * r * sp[:, blk]
        a = jnp.exp(log_a)
        a_ref[:, blk] = a
        u_ref[:, blk] = jnp.sqrt(1.0 - a * a) * (i * xc[:, blk])

    def step(t, h):
        tt = (L - 1 - t) if rev else t
        h = a_ref[pl.ds(tt, 1), :] * h + u_ref[pl.ds(tt, 1), :]
        o_ref[0, 0, pl.ds(tt, 1), :] = h
        return h

    h_ref[...] = lax.fori_loop(0, L, step, h_ref[...], unroll=8)


def _rglru_kernel(*refs):
    for rev in (False, True):
        @pl.when(pl.program_id(1) == int(rev))
        def _():
            _rglru_chunk(rev, *refs)


def _rglru(p2, cw, cb, wa, ba, wx, bx, lam):
    b, s, _ = p2.shape
    nch = s // ROW_TILE
    L = ROW_TILE
    R = RG_WIDTH
    vec = lambda a: a.reshape(2, 1, R)
    vspec = pl.BlockSpec((1, 1, R), lambda b, r, c: (r, 0, 0))
    wspec = pl.BlockSpec((1, RG_BLOCKS, RG_BLOCK_DIM, RG_BLOCK_DIM), lambda b, r, c: (r, 0, 0, 0))
    return pl.pallas_call(
        _rglru_kernel,
        out_shape=jax.ShapeDtypeStruct((2, b, s, R), F32),
        grid=(b, 2, nch),
        in_specs=[pl.BlockSpec((1, L, R), lambda b, r, c: (b, _scan_chunk(c, nch, r), 0)),
                  pl.BlockSpec((1, RG_CONV, R), lambda b, r, c: (r, 0, 0)),
                  vspec, wspec, vspec, wspec, vspec, vspec],
        out_specs=pl.BlockSpec((1, 1, L, R), lambda b, r, c: (r, b, _scan_chunk(c, nch, r), 0)),
        scratch_shapes=[pltpu.VMEM((L + 2 * HALO, R), F32),
                        pltpu.VMEM((L, R), F32),
                        pltpu.VMEM((L, R), F32),
                        pltpu.VMEM((1, R), F32)],
        compiler_params=_params(("parallel", "parallel", "arbitrary")),
        name="rglru",
    )(p2, cw, vec(cb), wa, vec(ba), wx, vec(bx), vec(lam))


def _gelu_tanh(x):
    return x * (0.5 * (1.0 + jnp.tanh(math.sqrt(2.0 / math.pi) * (x + 0.044715 * (x * x * x)))))


def _merge_kernel(mf_ref, mr_ref, rf_ref, rr_ref, o_ref, ry_ref, g_ref, y_ref):
    ml = mf_ref[0, 0] + mr_ref[0, 0]
    o = o_ref[0]
    g = g_ref[...]
    for h in range(ML_HEADS):
        blk = slice(h * ML_HEAD_DIM, (h + 1) * ML_HEAD_DIM)
        y = _rms(ml[:, blk], g[:, blk]) * _sigmoid(o[:, blk].astype(F32))
        y_ref[0, :, blk] = y.astype(y_ref.dtype)
    rg = rf_ref[0, 0] + rr_ref[0, 0]
    y_ref[0, :, ML_WIDTH:] = (rg * _gelu_tanh(ry_ref[0])).astype(y_ref.dtype)


def _merge(hml, hrg, p1, p2, ml_norm):
    _, b, s, _ = hml.shape
    L = ROW_TILE
    W = ML_WIDTH
    dspec = lambda r: pl.BlockSpec((1, 1, L, W), lambda b, i: (r, b, i, 0))
    return pl.pallas_call(
        _merge_kernel,
        out_shape=jax.ShapeDtypeStruct((b, s, 2 * W), BF16),
        grid=(b, s // L),
        in_specs=[dspec(0), dspec(1), dspec(0), dspec(1),
                  pl.BlockSpec((1, L, W), lambda b, i: (b, i, 3)),
                  pl.BlockSpec((1, L, W), lambda b, i: (b, i, 1)),
                  pl.BlockSpec((1, W), lambda b, i: (0, 0))],
        out_specs=pl.BlockSpec((1, L, 2 * W), lambda b, i: (b, i, 0)),
        compiler_params=_params(("parallel", "parallel")),
        name="merge",
    )(hml, hml, hrg, hrg, p1, p2, ml_norm.reshape(1, W))


def _rope_tables(s_ctx, t):
    rows = t // GRID_W
    row_ids = jnp.repeat(jnp.arange(rows), GRID_W).astype(F32)
    col_ids = jnp.tile(jnp.arange(GRID_W), rows).astype(F32)
    inv = ROPE_THETA ** (-jnp.arange(0, ROPE_AXIS_DIM, 2, dtype=F32) / ROPE_AXIS_DIM)
    ang_r = row_ids[:, None] * inv
    ang_c = col_ids[:, None] * inv
    ang = jnp.concatenate([ang_r, ang_r, ang_c, ang_c], axis=-1)
    cos = jnp.concatenate([jnp.ones((s_ctx, AT_HEAD_DIM), F32), jnp.cos(ang)], axis=0)
    sin = jnp.concatenate([jnp.zeros((s_ctx, AT_HEAD_DIM), F32), jnp.sin(ang)], axis=0)
    return cos, sin


def _rope(x, cos, sin):
    half = ROPE_AXIS_DIM // 2
    lane = lax.broadcasted_iota(jnp.int32, x.shape, 1)
    first = (lane % ROPE_AXIS_DIM) < half
    rot = jnp.where(first, -pltpu.roll(x, AT_HEAD_DIM - half, 1), pltpu.roll(x, half, 1))
    return x * cos + rot * sin


def _qkprep_kernel(p_ref, qn_ref, kn_ref, cos_ref, sin_ref, q_ref, k_ref, v_ref):
    cos = cos_ref[...]
    sin = sin_ref[...]
    Dh = AT_HEAD_DIM
    for h in range(AT_HEADS):
        x = _rope(_rms(p_ref[0, :, h * Dh:(h + 1) * Dh], qn_ref[...]), cos, sin)
        q_ref[0, :, h * Dh:(h + 1) * Dh] = (x * (Dh ** -0.5)).astype(q_ref.dtype)
    for h in range(AT_KV_HEADS):
        off = (AT_HEADS + h) * Dh
        x = _rope(_rms(p_ref[0, :, off:off + Dh], kn_ref[...]), cos, sin)
        k_ref[0, :, h * Dh:(h + 1) * Dh] = x.astype(k_ref.dtype)
    for h in range(AT_KV_HEADS):
        off = (AT_HEADS + AT_KV_HEADS + h) * Dh
        v_ref[0, :, 2 * h * Dh:(2 * h + 1) * Dh] = p_ref[0, :, off:off + Dh].astype(v_ref.dtype)
        v_ref[0, :, (2 * h + 1) * Dh:(2 * h + 2) * Dh] = jnp.ones((p_ref.shape[1], Dh), v_ref.dtype)


def _qkprep(p, q_norm, k_norm, cos, sin):
    b, s, w = p.shape
    L = ROW_TILE
    Dh = AT_HEAD_DIM
    qd, kd = AT_HEADS * Dh, AT_KV_HEADS * Dh
    return pl.pallas_call(
        _qkprep_kernel,
        out_shape=(jax.ShapeDtypeStruct((b, s, qd), BF16),
                   jax.ShapeDtypeStruct((b, s, kd), BF16),
                   jax.ShapeDtypeStruct((b, s, 2 * kd), BF16)),
        grid=(b, s // L),
        in_specs=[pl.BlockSpec((1, L, w), lambda b, i: (b, i, 0)),
                  pl.BlockSpec((1, Dh), lambda b, i: (0, 0)),
                  pl.BlockSpec((1, Dh), lambda b, i: (0, 0)),
                  pl.BlockSpec((L, Dh), lambda b, i: (i, 0)),
                  pl.BlockSpec((L, Dh), lambda b, i: (i, 0))],
        out_specs=(pl.BlockSpec((1, L, qd), lambda b, i: (b, i, 0)),
                   pl.BlockSpec((1, L, kd), lambda b, i: (b, i, 0)),
                   pl.BlockSpec((1, L, 2 * kd), lambda b, i: (b, i, 0))),
        compiler_params=_params(("parallel", "parallel")),
        name="qkprep",
    )(p, q_norm.reshape(1, Dh), k_norm.reshape(1, Dh), cos, sin)


def _attend(q_ref, k, v, o_ref):
    Dh = AT_HEAD_DIM
    for g in range(AT_GROUP):
        q = q_ref[0, :, g * Dh:(g + 1) * Dh]
        s = lax.dot_general(q, k, (((1,), (1,)), ((), ())), preferred_element_type=F32)
        e = jnp.exp((s - jnp.max(s, axis=-1, keepdims=True)).astype(BF16))
        o = jnp.dot(e, v, preferred_element_type=F32)
        o_ref[0, :, g * Dh:(g + 1) * Dh] = (o[:, :Dh] / o[:, Dh:]).astype(o_ref.dtype)


def _attn_kernel(q_ref, k_ref, v_ref, o_ref):
    i = pl.program_id(2)

    @pl.when(i == 0)
    def _():
        _attend(q_ref, k_ref[0, 0:ROW_TILE, :], v_ref[0, 0:ROW_TILE, :], o_ref)

    @pl.when(i > 0)
    def _():
        _attend(q_ref, k_ref[0], v_ref[0], o_ref)


def _attention(q, k, v):
    b, s, qd = q.shape
    L = ROW_TILE
    Dh = AT_HEAD_DIM
    gw = AT_GROUP * Dh
    return pl.pallas_call(
        _attn_kernel,
        out_shape=jax.ShapeDtypeStruct((b, s, qd), BF16),
        grid=(b, AT_KV_HEADS, s // L),
        in_specs=[pl.BlockSpec((1, L, gw), lambda b, h, i: (b, i, h)),
                  pl.BlockSpec((1, s, Dh), lambda b, h, i: (b, 0, h)),
                  pl.BlockSpec((1, s, 2 * Dh), lambda b, h, i: (b, 0, h))],
        out_specs=pl.BlockSpec((1, L, gw), lambda b, h, i: (b, i, h)),
        compiler_params=_params(("parallel", "parallel", "parallel")),
        name="attention",
    )(q, k, v)


def _cast_kernel(w_ref, o_ref):
    o_ref[...] = w_ref[...].astype(o_ref.dtype)


def _cast_bf16(w, layer):
    _, k, n = w.shape
    tk = math.gcd(k, 512)
    return pl.pallas_call(
        _cast_kernel,
        out_shape=jax.ShapeDtypeStruct((k, n), BF16),
        grid=(k // tk,),
        in_specs=[pl.BlockSpec((None, tk, n), lambda i: (layer, i, 0))],
        out_specs=pl.BlockSpec((tk, n), lambda i: (i, 0)),
        compiler_params=_params(("parallel",)),
        name="cast_bf16",
    )(w)


def _outproj_kernel(a_ref, w_ref, x_ref, g_ref, m_ref, wr_ref, xo_ref, h_ref, lg_ref):
    m = m_ref[0, 0]
    y = jnp.dot(a_ref[0], w_ref[...], preferred_element_type=F32)
    x = x_ref[0] + m[2:3] * y
    xo_ref[0] = x
    h = _rms(x, g_ref[...]) * (1.0 + m[4:5]) + m[3:4]
    hb = h.astype(h_ref.dtype)
    h_ref[0] = hb
    w2 = wr_ref[...]
    h_lo = (h - hb.astype(F32)).astype(BF16)
    both = jnp.dot(hb, w2, preferred_element_type=F32)
    lg_ref[0] = (both[:, :LANES] + both[:, LANES:]
                 + jnp.dot(h_lo, w2[:, :LANES], preferred_element_type=F32))


def _split_router(w_router):
    w = jnp.pad(w_router, ((0, 0), (0, 0), (0, LANES - w_router.shape[2])))
    hi = w.astype(BF16)
    lo = (w - hi.astype(F32)).astype(BF16)
    return jnp.concatenate([hi, lo], axis=2)


def _outproj(a, w, xs, g, mt, wr):
    b, s, d = xs.shape
    L = ROW_TILE
    tile = pl.BlockSpec((1, L, d), lambda b, i: (b, i, 0))
    return pl.pallas_call(
        _outproj_kernel,
        out_shape=(jax.ShapeDtypeStruct((b, s, d), F32),
                   jax.ShapeDtypeStruct((b, s, d), BF16),
                   jax.ShapeDtypeStruct((b, s, LANES), F32)),
        grid=(b, s // L),
        in_specs=[tile, pl.BlockSpec((d, d), lambda b, i: (0, 0)), tile,
                  pl.BlockSpec((1, d), lambda b, i: (0, 0)), _mod_spec(d),
                  pl.BlockSpec((d, 2 * LANES), lambda b, i: (0, 0))],
        out_specs=(tile, tile, pl.BlockSpec((1, L, LANES), lambda b, i: (b, i, 0))),
        compiler_params=_params(("parallel", "parallel")),
        name="outproj",
    )(a, w, xs, g.reshape(1, d), mt, wr)


def _excl_prefix(mask):
    e, n = mask.shape
    groups = n // LANES
    r = lax.broadcasted_iota(jnp.int32, (LANES, LANES), 0)
    c = lax.broadcasted_iota(jnp.int32, (LANES, LANES), 1)
    upper = jnp.where(r < c, 1.0, 0.0).astype(BF16)
    stk = jnp.concatenate([mask[:, g * LANES:(g + 1) * LANES] for g in range(groups)], axis=0)
    within = jnp.dot(stk.astype(BF16), upper, preferred_element_type=F32)
    tot = jnp.sum(stk, axis=1, keepdims=True)
    outs = []
    off = jnp.zeros((e, 1), F32)
    for g in range(groups):
        outs.append(within[g * e:(g + 1) * e] + off)
        off = off + tot[g * e:(g + 1) * e]
    return jnp.concatenate(outs, axis=1)


def _top_slots(aff, cap):
    v = lax.bitcast_convert_type(aff, jnp.int32)
    thr = jnp.zeros((aff.shape[0], 1), jnp.int32)
    for bit in range(30, -1, -1):
        cand = thr | (1 << bit)
        cnt = jnp.sum(jnp.where(v >= cand, 1.0, 0.0), axis=1, keepdims=True)
        thr = jnp.where(cnt >= cap, cand, thr)
    gt = jnp.where(v > thr, 1.0, 0.0)
    eq = jnp.where(v == thr, 1.0, 0.0)
    need = cap - jnp.sum(gt, axis=1, keepdims=True)
    sel = gt + eq * jnp.where(_excl_prefix(eq) < need, 1.0, 0.0)
    return jnp.where(sel > 0.0, _excl_prefix(sel), -1.0)


def _route_kernel(lg_ref, pos_ref, aff_ref, post_ref):
    s = lg_ref.shape[1]
    E = N_EXPERTS
    lt = lg_ref[0].T[0:E, :]
    ex = jnp.exp(lt - jnp.max(lt, axis=0, keepdims=True))
    aff = ex / jnp.sum(ex, axis=0, keepdims=True)
    aff_ref[0] = aff
    pos = jnp.concatenate(
        [_top_slots(aff[:, :ROW_TILE], EC_FACTOR * ROW_TILE // E),
         _top_slots(aff[:, ROW_TILE:], EC_FACTOR * (s - ROW_TILE) // E)], axis=1)
    pos_ref[0] = pos
    padded = jnp.concatenate([pos, jnp.full((LANES - E, s), -1.0, F32)], axis=0)
    post_ref[0] = padded.T


def _route(logits):
    b, s, _ = logits.shape
    E = N_EXPERTS
    return pl.pallas_call(
        _route_kernel,
        out_shape=(jax.ShapeDtypeStruct((b, E, s), F32),
                   jax.ShapeDtypeStruct((b, E, s), F32),
                   jax.ShapeDtypeStruct((b, s, LANES), F32)),
        grid=(b,),
        in_specs=[pl.BlockSpec((1, s, LANES), lambda b: (b, 0, 0))],
        out_specs=(pl.BlockSpec((1, E, s), lambda b: (b, 0, 0)),
                   pl.BlockSpec((1, E, s), lambda b: (b, 0, 0)),
                   pl.BlockSpec((1, s, LANES), lambda b: (b, 0, 0))),
        compiler_params=_params(("parallel",)),
        name="route",
    )(logits)


def _gather_kernel(h_ref, pos_ref, aff_ref, xx_ref, xc_ref, gx_ref, gc_ref):
    s = h_ref.shape[1]
    group = xx_ref.shape[0]
    for lo, n, x_ref, g_ref in ((0, ROW_TILE, xc_ref, gc_ref), (ROW_TILE, s - ROW_TILE, xx_ref, gx_ref)):
        cap = x_ref.shape[1]
        slot = lax.broadcasted_iota(jnp.int32, (cap, n), 0).astype(F32)
        hits = [pos_ref[0, j, :, lo:lo + n] == slot for j in range(group)]
        onehot = jnp.concatenate([jnp.where(hit, 1.0, 0.0).astype(BF16) for hit in hits], axis=0)
        rows = jnp.dot(onehot, h_ref[0, lo:lo + n, :], preferred_element_type=F32)
        for j in range(group):
            x_ref[j] = rows[j * cap:(j + 1) * cap].astype(x_ref.dtype)
            g_ref[j] = jnp.sum(jnp.where(hits[j], aff_ref[0, j, :, lo:lo + n], 0.0), axis=1, keepdims=True)


def _gather(h, pos, aff):
    b, s, d = h.shape
    E = N_EXPERTS
    G = GATHER_GROUP
    cap_c = EC_FACTOR * ROW_TILE // E
    cap_x = EC_FACTOR * (s - ROW_TILE) // E
    row = pl.BlockSpec((1, G, 1, s), lambda b, e: (b, e, 0, 0))
    return pl.pallas_call(
        _gather_kernel,
        out_shape=(jax.ShapeDtypeStruct((E, b * cap_x, d), BF16),
                   jax.ShapeDtypeStruct((E, b * cap_c, d), BF16),
                   jax.ShapeDtypeStruct((E, b * cap_x, 1), F32),
                   jax.ShapeDtypeStruct((E, b * cap_c, 1), F32)),
        grid=(b, E // G),
        in_specs=[pl.BlockSpec((1, s, d), lambda b, e: (b, 0, 0)), row, row],
        out_specs=(pl.BlockSpec((G, cap_x, d), lambda b, e: (e, b, 0)),
                   pl.BlockSpec((G, cap_c, d), lambda b, e: (e, b, 0)),
                   pl.BlockSpec((G, cap_x, 1), lambda b, e: (e, b, 0)),
                   pl.BlockSpec((G, cap_c, 1), lambda b, e: (e, b, 0))),
        compiler_params=_params(("parallel", "arbitrary")),
        name="gather",
    )(h, pos.reshape(b, E, 1, s), aff.reshape(b, E, 1, s))


def _ffn_kernel(n_sets, *refs):
    x_refs, g_refs = refs[:n_sets], refs[n_sets:2 * n_sets]
    wg_ref, wu_ref, wd_ref = refs[2 * n_sets:2 * n_sets + 3]
    y_refs, a_refs = refs[2 * n_sets + 3:3 * n_sets + 3], refs[3 * n_sets + 3:]
    step = pl.program_id(1)
    ax_ref = a_refs[0]
    nf = ax_ref.shape[0]
    pairs = tuple(zip(x_refs, a_refs, g_refs, y_refs))

    @pl.when(step < nf)
    def _():
        wg = wg_ref[0].astype(BF16)
        wu = wu_ref[0].astype(BF16)
        for x_ref, a_ref, _, _ in pairs:
            x = x_ref[0]
            g = jnp.dot(x, wg, preferred_element_type=F32)
            u = jnp.dot(x, wu, preferred_element_type=F32)
            a_ref[step] = (g * _sigmoid(g) * u).astype(BF16)

    @pl.when(step >= nf)
    def _():
        wd = wd_ref[0].astype(BF16)
        tf = ax_ref.shape[2]
        for _, a_ref, g_ref, y_ref in pairs:
            y = jnp.dot(a_ref[0], wd[0:tf], preferred_element_type=F32)
            for f in range(1, nf):
                y = y + jnp.dot(a_ref[f], wd[f * tf:(f + 1) * tf], preferred_element_type=F32)
            y_ref[0] = (y * g_ref[0]).astype(y_ref.dtype)


def _ffn(xs, gs, w_gate, w_up, w_down, layer):
    E, _, d = xs[0].shape
    rows = [x.shape[1] for x in xs]
    ff = w_gate.shape[3]
    nf = ff // FFN_HIDDEN_TILE
    nd = d // FFN_OUT_TILE
    xspec = lambda r, w: pl.BlockSpec((1, r, w), lambda e, s: (e, 0, 0))
    hid = lambda e, s: (layer, e, 0, jnp.minimum(s, nf - 1))
    out_tile = lambda s: jnp.maximum(s - nf, 0)
    yspec = lambda r: pl.BlockSpec((1, r, FFN_OUT_TILE), lambda e, s: (e, 0, out_tile(s)))
    return pl.pallas_call(
        functools.partial(_ffn_kernel, len(xs)),
        out_shape=tuple(jax.ShapeDtypeStruct((E, r, d), BF16) for r in rows),
        grid=(E, nf + nd),
        in_specs=[xspec(r, d) for r in rows] + [xspec(r, 1) for r in rows] + [
            pl.BlockSpec((None, 1, d, FFN_HIDDEN_TILE), hid),
            pl.BlockSpec((None, 1, d, FFN_HIDDEN_TILE), hid),
            pl.BlockSpec((None, 1, ff, FFN_OUT_TILE), lambda e, s: (layer, e, 0, out_tile(s)))],
        out_specs=tuple(yspec(r) for r in rows),
        scratch_shapes=[pltpu.VMEM((nf, r, FFN_HIDDEN_TILE), BF16) for r in rows],
        compiler_params=_params(("parallel", "arbitrary")),
        name="expert_ffn",
    )(*xs, *gs, w_gate, w_up, w_down)


def _combine(pt, y_ref):
    cap = y_ref.shape[1]
    n = pt.shape[0]
    slot = lax.broadcasted_iota(jnp.int32, (n, cap), 1).astype(F32)
    acc = jnp.zeros((n, y_ref.shape[2]), F32)
    for e in range(N_EXPERTS):
        onehot = jnp.where(pt[:, e:e + 1] == slot, 1.0, 0.0).astype(BF16)
        acc = acc + jnp.dot(onehot, y_ref[e], preferred_element_type=F32)
    return acc


def _scatter_kernel(x_ref, pt_ref, yx_ref, yc_ref, m_ref, gn_ref, mn_ref, o_ref, h_ref):
    i = pl.program_id(1)
    gate = m_ref[0, 0][5:6]
    mn = mn_ref[0, 0]

    def finish(acc):
        x = x_ref[0] + gate * acc
        o_ref[0] = x
        h_ref[0] = (_rms(x, gn_ref[...]) * (1.0 + mn[1:2]) + mn[0:1]).astype(h_ref.dtype)

    @pl.when(i == 0)
    def _():
        finish(_combine(pt_ref[0], yc_ref))

    @pl.when(i > 0)
    def _():
        finish(_combine(pt_ref[0], yx_ref))


def _scatter(xs, post, yx, yc, mt, g_next, mt_next):
    b, s, d = xs.shape
    E = N_EXPERTS
    L = ROW_TILE
    cap_x = yx.shape[1] // b
    cap_c = yc.shape[1] // b
    tile = pl.BlockSpec((1, L, d), lambda b, i: (b, i, 0))
    return pl.pallas_call(
        _scatter_kernel,
        out_shape=(jax.ShapeDtypeStruct((b, s, d), F32), jax.ShapeDtypeStruct((b, s, d), BF16)),
        grid=(b, s // L),
        in_specs=[tile,
                  pl.BlockSpec((1, L, LANES), lambda b, i: (b, i, 0)),
                  pl.BlockSpec((E, cap_x, d), lambda b, i: (0, b, 0)),
                  pl.BlockSpec((E, cap_c, d), lambda b, i: (0, b, 0)),
                  _mod_spec(d),
                  pl.BlockSpec((1, d), lambda b, i: (0, 0)),
                  _mod_spec(d)],
        out_specs=(tile, tile),
        compiler_params=_params(("parallel", "arbitrary")),
        name="scatter",
    )(xs, post, yx, yc, mt, g_next.reshape(1, d), mt_next)


def _scatter_last_kernel(x_ref, pt_ref, yx_ref, m_ref, o_ref):
    o_ref[0] = x_ref[0] + m_ref[0, 0][5:6] * _combine(pt_ref[0], yx_ref)


def _scatter_last(xs, post, yx, mt):
    b, s, d = xs.shape
    E = N_EXPERTS
    L = ROW_TILE
    cap_x = yx.shape[1] // b
    return pl.pallas_call(
        _scatter_last_kernel,
        out_shape=jax.ShapeDtypeStruct((b, s - L, d), F32),
        grid=(b, s // L - 1),
        in_specs=[pl.BlockSpec((1, L, d), lambda b, i: (b, i + 1, 0)),
                  pl.BlockSpec((1, L, LANES), lambda b, i: (b, i + 1, 0)),
                  pl.BlockSpec((E, cap_x, d), lambda b, i: (0, b, 0)),
                  pl.BlockSpec((1, 1, 6, d), lambda b, i: (b, 1, 0, 0))],
        out_specs=pl.BlockSpec((1, L, d), lambda b, i: (b, i, 0)),
        compiler_params=_params(("parallel", "arbitrary")),
        name="scatter_last",
    )(xs, post, yx, mt)


def kernel(x, c, ctx, c_ctx, w_mod, b_mod, norm_mix, norm_ffn, ab_w_in, ab_gate_b, ml_norm,
           rg_conv_w, rg_conv_b, rg_wa, rg_ba, rg_wx, rg_bx, rg_lam, ab_w_out,
           at_w_qkv, at_q_norm, at_k_norm, at_w_o,
           moe_w_router, moe_w_gate, moe_w_up, moe_w_down):
    B, T, D = x.shape
    n_ctx = ctx.shape[1]
    assert n_ctx == ROW_TILE and T % ROW_TILE == 0 and B <= 7
    S = n_ctx + T
    depth = w_mod.shape[0]
    E = moe_w_router.shape[2]

    c8 = jnp.concatenate([c, c_ctx[None], jnp.zeros((7 - B, D), F32)], axis=0)
    modv = _mods(c8, w_mod, b_mod).reshape(depth, 8, 6, D)
    mod_c = jnp.broadcast_to(modv[:, B][:, None], (depth, B, 6, D))
    mtab = jnp.stack([mod_c, modv[:, :B]], axis=2)
    cos, sin = _rope_tables(n_ctx, T)
    wr2 = _split_router(moe_w_router)
    w_in_t = jnp.swapaxes(ab_w_in, 1, 2)

    xs, h = _norm_mod(ctx, x, norm_mix[0], mtab[0])
    for l in range(depth):
        j = l // 2
        mt = mtab[l]
        h = h.reshape(B * S, D)
        if l % 2 == 0:
            g0 = 4 * ML_WIDTH
            p1 = _matmul_t(h, w_in_t, j, 0, g0, tn=512, out_dtype=BF16, name="ab_in_qkvo").reshape(B, S, g0)
            p2 = _matmul_t(h, w_in_t, j, g0 + N_GATES, 2 * RG_WIDTH, tn=512,
                           name="ab_in_rg").reshape(B, S, 2 * RG_WIDTH)
            gates = _matmul_t(h, w_in_t, j, g0, N_GATES, tn=N_GATES, name="ab_in_gates").reshape(B, S, N_GATES)
            gcol = gates + ab_gate_b[j].reshape(-1)
            hml = _mlstm(p1, gcol, jnp.swapaxes(gcol, 1, 2))
            hrg = _rglru(p2, rg_conv_w[j], rg_conv_b[j], rg_wa[j], rg_ba[j], rg_wx[j], rg_bx[j], rg_lam[j])
            mixed = _merge(hml, hrg, p1, p2, ml_norm[j])
            w_out = _cast_bf16(ab_w_out, j)
        else:
            wq = at_w_qkv.shape[2]
            p = _matmul(h, at_w_qkv, wq, tn=512, layer=j, name="at_qkv").reshape(B, S, wq)
            q, k, v = _qkprep(p, at_q_norm[j], at_k_norm[j], cos, sin)
            mixed = _attention(q, k, v)
            w_out = _cast_bf16(at_w_o, j)
        xs, h2, logits = _outproj(mixed, w_out, xs, norm_ffn[l], mt, wr2[l])
        pos, aff, post = _route(logits)
        xx, xc, gx, gc = _gather(h2, pos, aff)
        if l + 1 == depth:
            yx, = _ffn((xx,), (gx,), moe_w_gate, moe_w_up, moe_w_down, l)
            return _scatter_last(xs, post, yx, mt)
        yx, yc = _ffn((xx, xc), (gx, gc), moe_w_gate, moe_w_up, moe_w_down, l)
        xs, h = _scatter(xs, post, yx, yc, mt, norm_mix[l + 1], mtab[l + 1])
```

```python
import functools
import math

import jax
import jax.numpy as jnp
from jax import lax
from jax.experimental import pallas as pl
from jax.experimental.pallas import tpu as pltpu

F32 = jnp.float32
BF16 = jnp.bfloat16
HIGHEST = lax.Precision.HIGHEST

NORM_EPS = 1e-6
GRID_W = 64
ROW_TILE = 256
LANES = 128
ML_HEADS = 8
ML_HEAD_DIM = 128
ML_WIDTH = ML_HEADS * ML_HEAD_DIM
N_GATES = 4 * ML_HEADS
RG_WIDTH = 1024
RG_BLOCKS = 8
RG_BLOCK_DIM = RG_WIDTH // RG_BLOCKS
RG_CONV = 4
RG_C = 8.0
AT_HEAD_DIM = 128
AT_HEADS = 16
AT_KV_HEADS = 4
AT_GROUP = AT_HEADS // AT_KV_HEADS
ROPE_AXIS_DIM = AT_HEAD_DIM // 2
ROPE_THETA = 10000.0
N_EXPERTS = 16
EC_FACTOR = 2
MM_ROWS = 2304
FFN_HIDDEN_TILE = 256
FFN_OUT_TILE = 512
GATHER_GROUP = 4
HALO = 8
VMEM_LIMIT = 56 * 1024 * 1024


def _params(sem, vmem=VMEM_LIMIT):
    return pltpu.CompilerParams(dimension_semantics=sem, vmem_limit_bytes=vmem)


def _sigmoid(x):
    return 0.5 * jnp.tanh(0.5 * x) + 0.5


def _softplus(x):
    return jnp.maximum(x, 0.0) + jnp.log1p(jnp.exp(-jnp.abs(x)))


def _log_sigmoid(x):
    return -_softplus(-x)


def _mods_kernel(c_ref, w_ref, b_ref, o_ref):
    c = c_ref[...]
    a = c * _sigmoid(c)
    o_ref[0] = jnp.dot(a.astype(BF16), w_ref[0].astype(BF16), preferred_element_type=F32) + b_ref[0]


def _mods(c8, w_mod, b_mod):
    depth, d, n = w_mod.shape
    tn = 1024
    return pl.pallas_call(
        _mods_kernel,
        out_shape=jax.ShapeDtypeStruct((depth, 8, n), F32),
        grid=(depth, n // tn),
        in_specs=[pl.BlockSpec((8, d), lambda l, j: (0, 0)),
                  pl.BlockSpec((1, d, tn), lambda l, j: (l, 0, j)),
                  pl.BlockSpec((1, 1, tn), lambda l, j: (l, 0, j))],
        out_specs=pl.BlockSpec((1, 8, tn), lambda l, j: (l, 0, j)),
        compiler_params=_params(("parallel", "parallel")),
        name="mods",
    )(c8, w_mod, b_mod.reshape(depth, 1, n))


def _mod_spec(d):
    return pl.BlockSpec((1, 1, 6, d), lambda b, i: (b, jnp.minimum(i, 1), 0, 0))


def _rms(x, g):
    return x * lax.rsqrt(jnp.mean(x * x, axis=-1, keepdims=True) + NORM_EPS) * g


def _norm_kernel(c_ref, x_ref, g_ref, m_ref, xs_ref, h_ref):
    m = m_ref[0, 0]

    def emit(x):
        xs_ref[0] = x
        h_ref[0] = (_rms(x, g_ref[...]) * (1.0 + m[1:2]) + m[0:1]).astype(h_ref.dtype)

    @pl.when(pl.program_id(1) == 0)
    def _():
        emit(c_ref[0])

    @pl.when(pl.program_id(1) > 0)
    def _():
        emit(x_ref[0])


def _norm_mod(ctx, x, g, mt):
    b, t, d = x.shape
    L = ROW_TILE
    s = L + t
    tile = pl.BlockSpec((1, L, d), lambda b, i: (b, i, 0))
    return pl.pallas_call(
        _norm_kernel,
        out_shape=(jax.ShapeDtypeStruct((b, s, d), F32), jax.ShapeDtypeStruct((b, s, d), BF16)),
        grid=(b, s // L),
        in_specs=[pl.BlockSpec((1, L, d), lambda b, i: (b, 0, 0)),
                  pl.BlockSpec((1, L, d), lambda b, i: (b, jnp.maximum(i - 1, 0), 0)),
                  pl.BlockSpec((1, d), lambda b, i: (0, 0)),
                  _mod_spec(d)],
        out_specs=(tile, tile),
        compiler_params=_params(("parallel", "arbitrary")),
        name="norm_mod",
    )(ctx, x, g.reshape(1, d), mt)


def _mm_kernel(a_ref, w_ref, o_ref):
    o_ref[...] = jnp.dot(a_ref[...], w_ref[...].astype(BF16),
                         preferred_element_type=F32).astype(o_ref.dtype)


def _matmul(a, w, n, *, tn, layer=None, out_dtype=F32, name="matmul"):
    m, k = a.shape
    tm = math.gcd(m, MM_ROWS)
    if layer is None:
        wspec = pl.BlockSpec((k, tn), lambda i, j: (0, j))
    else:
        wspec = pl.BlockSpec((None, k, tn), lambda i, j: (layer, 0, j))
    return pl.pallas_call(
        _mm_kernel,
        out_shape=jax.ShapeDtypeStruct((m, n), out_dtype),
        grid=(m // tm, n // tn),
        in_specs=[pl.BlockSpec((tm, k), lambda i, j: (i, 0)), wspec],
        out_specs=pl.BlockSpec((tm, tn), lambda i, j: (i, j)),
        compiler_params=_params(("parallel", "parallel")),
        name=name,
    )(a, w)


def _mm_t_kernel(a_ref, w_ref, o_ref):
    o_ref[...] = lax.dot_general(a_ref[...], w_ref[0].astype(BF16), (((1,), (1,)), ((), ())),
                                 preferred_element_type=F32).astype(o_ref.dtype)


def _matmul_t(a, w_t, layer, row0, n, *, tn, out_dtype=F32, name="matmul_t"):
    m, k = a.shape
    tm = math.gcd(m, MM_ROWS)
    assert row0 % 8 == 0 and n % tn == 0
    wspec = pl.BlockSpec((pl.Element(1), pl.Element(tn), pl.Element(k)),
                         lambda i, j: (layer, pl.multiple_of(row0 + tn * j, 8), 0))
    return pl.pallas_call(
        _mm_t_kernel,
        out_shape=jax.ShapeDtypeStruct((m, n), out_dtype),
        grid=(m // tm, n // tn),
        in_specs=[pl.BlockSpec((tm, k), lambda i, j: (i, 0)), wspec],
        out_specs=pl.BlockSpec((tm, tn), lambda i, j: (i, j)),
        compiler_params=_params(("parallel", "parallel")),
        name=name,
    )(a, w_t)


def _scan_chunk(c, n_chunks, rev):
    return jnp.where(jnp.logical_or(c == 0, rev == 0), c, n_chunks - c)


def _mlstm_head(rev, head, vis, gc, gr, lfr, bc, br, q_ref, k_ref, v_ref, o_ref, c_ref, n_ref, m_ref):
    Dh = ML_HEAD_DIM
    blk = slice(head * Dh, (head + 1) * Dh)
    gi = int(rev) * 2 * ML_HEADS + head
    gf = gi + ML_HEADS
    nt = (((1,), (1,)), ((), ()))
    i_row = gr[gi:gi + 1, :]
    f_row = lfr[gf:gf + 1, :]
    b_row = br[gf:gf + 1, :]
    src = gc[:, gi:gi + 1] - bc[:, gf:gf + 1]
    d = jnp.where(vis, b_row + src, -jnp.inf)
    m_prev = m_ref[head:head + 1, :]
    inter = b_row + m_prev
    m_t = jnp.maximum(inter, jnp.max(d, axis=0, keepdims=True))
    w_inter = jnp.exp(inter - m_t)

    qb = q_ref[0, :, blk]
    kb = k_ref[0, :, blk]
    vt = v_ref[0, :, blk].astype(F32).T
    c = c_ref[head]
    n_row = n_ref[head:head + 1, :]
    n8 = jnp.broadcast_to(n_row, (8, Dh)).astype(BF16)
    log_scale = -0.5 * math.log(Dh)
    s = lax.dot_general(kb, qb, nt, preferred_element_type=F32) * jnp.exp(d - (m_t - log_scale))
    num = (w_inter * lax.dot_general(c.astype(BF16), qb, nt, preferred_element_type=F32)
           + jnp.dot(vt.astype(BF16), s.astype(BF16), preferred_element_type=F32))
    den = (w_inter * lax.dot_general(n8, qb, nt, preferred_element_type=F32)[0:1]
           + jnp.sum(s, axis=0, keepdims=True))
    o_ref[0, 0, :, blk] = (num / jnp.maximum(jnp.abs(den), jnp.exp(-m_t))).T

    b_end = jnp.sum(f_row, axis=1, keepdims=True)
    g_row = b_end - b_row + i_row
    m_new = jnp.maximum(b_end + m_prev, jnp.max(g_row, axis=1, keepdims=True))
    w_old = jnp.exp(b_end + m_prev - m_new)
    w_tok = jnp.exp(g_row - m_new) * (Dh ** -0.5)
    c_ref[head] = w_old * c + jnp.dot((vt * w_tok).astype(BF16), kb, preferred_element_type=F32)
    w8 = jnp.broadcast_to(w_tok, (8, w_tok.shape[1])).astype(BF16)
    n_ref[head:head + 1, :] = w_old * n_row + jnp.dot(w8, kb, preferred_element_type=F32)[0:1]
    m_ref[head:head + 1, :] = m_new


def _mlstm_kernel(q_ref, k_ref, v_ref, gc_ref, gr_ref, o_ref, c_ref, n_ref, m_ref):
    L = ROW_TILE

    @pl.when(pl.program_id(2) == 0)
    def _():
        c_ref[...] = jnp.zeros_like(c_ref)
        n_ref[...] = jnp.zeros_like(n_ref)
        m_ref[...] = jnp.zeros_like(m_ref)

    row = lax.broadcasted_iota(jnp.int32, (L, L), 0)
    col = lax.broadcasted_iota(jnp.int32, (L, L), 1)
    gc = gc_ref[0]
    gr = gr_ref[0]
    lfc = _log_sigmoid(gc)
    lfr = _log_sigmoid(gr)
    for rev in (False, True):
        @pl.when(pl.program_id(1) == int(rev))
        def _():
            vis_qs = (col >= row) if rev else (col <= row)
            vis_sq = (row >= col) if rev else (row <= col)
            bc = jnp.dot(jnp.where(vis_qs, 1.0, 0.0), lfc, preferred_element_type=F32, precision=HIGHEST)
            br = jnp.dot(lfr, jnp.where(vis_sq, 1.0, 0.0), preferred_element_type=F32, precision=HIGHEST)
            for head in range(ML_HEADS):
                _mlstm_head(rev, head, vis_sq, gc, gr, lfr, bc, br, q_ref, k_ref, v_ref, o_ref,
                            c_ref, n_ref, m_ref)


def _mlstm(p1, gcol, grow):
    b, s, _ = p1.shape
    nch = s // ROW_TILE
    L = ROW_TILE
    W = ML_WIDTH

    def qkv_spec(off):
        return pl.BlockSpec((1, L, W), lambda b, r, c: (b, _scan_chunk(c, nch, r), off))

    return pl.pallas_call(
        _mlstm_kernel,
        out_shape=jax.ShapeDtypeStruct((2, b, s, W), F32),
        grid=(b, 2, nch),
        in_specs=[qkv_spec(0), qkv_spec(1), qkv_spec(2),
                  pl.BlockSpec((1, L, N_GATES), lambda b, r, c: (b, _scan_chunk(c, nch, r), 0)),
                  pl.BlockSpec((1, N_GATES, L), lambda b, r, c: (b, 0, _scan_chunk(c, nch, r)))],
        out_specs=pl.BlockSpec((1, 1, L, W), lambda b, r, c: (r, b, _scan_chunk(c, nch, r), 0)),
        scratch_shapes=[pltpu.VMEM((ML_HEADS, ML_HEAD_DIM, ML_HEAD_DIM), F32),
                        pltpu.VMEM((ML_HEADS, ML_HEAD_DIM), F32),
                        pltpu.VMEM((ML_HEADS, 1), F32)],
        compiler_params=_params(("parallel", "parallel", "arbitrary")),
        name="mlstm",
    )(p1, p1, p1, gcol, grow)


def _rglru_chunk(rev, x_ref, cw_ref, cb_ref, wa_ref, ba_ref, wx_ref, bx_ref, lam_ref, o_ref,
                 ext_ref, a_ref, u_ref, h_ref):
    L = ROW_TILE
    c = pl.program_id(2)
    halo = slice(HALO + L, 2 * HALO + L) if rev else slice(0, HALO)

    @pl.when(c <= 1)
    def _():
        ext_ref[halo, :] = jnp.zeros((HALO, RG_WIDTH), F32)

    @pl.when(c == 0)
    def _():
        h_ref[...] = jnp.zeros_like(h_ref)

    x = x_ref[0]
    ext_ref[HALO:HALO + L, :] = x
    xc = cb_ref[0]
    for j in range(RG_CONV):
        k = RG_CONV - 1 - j
        lo = HALO + k if rev else HALO - k
        xc = xc + cw_ref[0, j:j + 1, :] * ext_ref[lo:lo + L, :]
    ext_ref[halo, :] = x[0:HALO] if rev else x[L - HALO:L]

    sp = _softplus(-lam_ref[0])
    xcb = xc.astype(BF16)
    for n in range(RG_BLOCKS):
        blk = slice(n * RG_BLOCK_DIM, (n + 1) * RG_BLOCK_DIM)
        xb = xcb[:, blk]
        r = _sigmoid(jnp.dot(xb, wa_ref[0, n].astype(BF16), preferred_element_type=F32)
                     + ba_ref[0, :, blk])
        i = _sigmoid(jnp.dot(xb, wx_ref[0, n].astype(BF16), preferred_element_type=F32)
                     + bx_ref[0, :, blk])
        log_a = -RG_C * r * sp[:, blk]
        a = jnp.exp(log_a)
        a_ref[:, blk] = a
        u_ref[:, blk] = jnp.sqrt(1.0 - a * a) * (i * xc[:, blk])

    def step(t, h):
        tt = (L - 1 - t) if rev else t
        h = a_ref[pl.ds(tt, 1), :] * h + u_ref[pl.ds(tt, 1), :]
        o_ref[0, 0, pl.ds(tt, 1), :] = h
        return h

    h_ref[...] = lax.fori_loop(0, L, step, h_ref[...], unroll=8)


def _rglru_kernel(*refs):
    for rev in (False, True):
        @pl.when(pl.program_id(1) == int(rev))
        def _():
            _rglru_chunk(rev, *refs)


def _rglru(p2, cw, cb, wa, ba, wx, bx, lam):
    b, s, _ = p2.shape
    nch = s // ROW_TILE
    L = ROW_TILE
    R = RG_WIDTH
    vec = lambda a: a.reshape(2, 1, R)
    vspec = pl.BlockSpec((1, 1, R), lambda b, r, c: (r, 0, 0))
    wspec = pl.BlockSpec((1, RG_BLOCKS, RG_BLOCK_DIM, RG_BLOCK_DIM), lambda b, r, c: (r, 0, 0, 0))
    return pl.pallas_call(
        _rglru_kernel,
        out_shape=jax.ShapeDtypeStruct((2, b, s, R), F32),
        grid=(b, 2, nch),
        in_specs=[pl.BlockSpec((1, L, R), lambda b, r, c: (b, _scan_chunk(c, nch, r), 0)),
                  pl.BlockSpec((1, RG_CONV, R), lambda b, r, c: (r, 0, 0)),
                  vspec, wspec, vspec, wspec, vspec, vspec],
        out_specs=pl.BlockSpec((1, 1, L, R), lambda b, r, c: (r, b, _scan_chunk(c, nch, r), 0)),
        scratch_shapes=[pltpu.VMEM((L + 2 * HALO, R), F32),
                        pltpu.VMEM((L, R), F32),
                        pltpu.VMEM((L, R), F32),
                        pltpu.VMEM((1, R), F32)],
        compiler_params=_params(("parallel", "parallel", "arbitrary")),
        name="rglru",
    )(p2, cw, vec(cb), wa, vec(ba), wx, vec(bx), vec(lam))


def _gelu_tanh(x):
    return x * (0.5 * (1.0 + jnp.tanh(math.sqrt(2.0 / math.pi) * (x + 0.044715 * (x * x * x)))))


def _merge_kernel(mf_ref, mr_ref, rf_ref, rr_ref, o_ref, ry_ref, g_ref, y_ref):
    ml = mf_ref[0, 0] + mr_ref[0, 0]
    o = o_ref[0]
    g = g_ref[...]
    for h in range(ML_HEADS):
        blk = slice(h * ML_HEAD_DIM, (h + 1) * ML_HEAD_DIM)
        y = _rms(ml[:, blk], g[:, blk]) * _sigmoid(o[:, blk].astype(F32))
        y_ref[0, :, blk] = y.astype(y_ref.dtype)
    rg = rf_ref[0, 0] + rr_ref[0, 0]
    y_ref[0, :, ML_WIDTH:] = (rg * _gelu_tanh(ry_ref[0])).astype(y_ref.dtype)


def _merge(hml, hrg, p1, p2, ml_norm):
    _, b, s, _ = hml.shape
    L = ROW_TILE
    W = ML_WIDTH
    dspec = lambda r: pl.BlockSpec((1, 1, L, W), lambda b, i: (r, b, i, 0))
    return pl.pallas_call(
        _merge_kernel,
        out_shape=jax.ShapeDtypeStruct((b, s, 2 * W), BF16),
        grid=(b, s // L),
        in_specs=[dspec(0), dspec(1), dspec(0), dspec(1),
                  pl.BlockSpec((1, L, W), lambda b, i: (b, i, 3)),
                  pl.BlockSpec((1, L, W), lambda b, i: (b, i, 1)),
                  pl.BlockSpec((1, W), lambda b, i: (0, 0))],
        out_specs=pl.BlockSpec((1, L, 2 * W), lambda b, i: (b, i, 0)),
        compiler_params=_params(("parallel", "parallel")),
        name="merge",
    )(hml, hml, hrg, hrg, p1, p2, ml_norm.reshape(1, W))


def _rope_tables(s_ctx, t):
    rows = t // GRID_W
    row_ids = jnp.repeat(jnp.arange(rows), GRID_W).astype(F32)
    col_ids = jnp.tile(jnp.arange(GRID_W), rows).astype(F32)
    inv = ROPE_THETA ** (-jnp.arange(0, ROPE_AXIS_DIM, 2, dtype=F32) / ROPE_AXIS_DIM)
    ang_r = row_ids[:, None] * inv
    ang_c = col_ids[:, None] * inv
    ang = jnp.concatenate([ang_r, ang_r, ang_c, ang_c], axis=-1)
    cos = jnp.concatenate([jnp.ones((s_ctx, AT_HEAD_DIM), F32), jnp.cos(ang)], axis=0)
    sin = jnp.concatenate([jnp.zeros((s_ctx, AT_HEAD_DIM), F32), jnp.sin(ang)], axis=0)
    return cos, sin


def _rope(x, cos, sin):
    half = ROPE_AXIS_DIM // 2
    lane = lax.broadcasted_iota(jnp.int32, x.shape, 1)
    first = (lane % ROPE_AXIS_DIM) < half
    rot = jnp.where(first, -pltpu.roll(x, AT_HEAD_DIM - half, 1), pltpu.roll(x, half, 1))
    return x * cos + rot * sin


def _rope_mxu(x, cos, sin, perm2):
    hi = x.astype(BF16)
    lo = (x - hi.astype(F32)).astype(BF16)
    rot = jnp.dot(jnp.concatenate([hi, lo], axis=1), perm2, preferred_element_type=F32)
    return x * cos + rot * sin


def _qkprep_kernel(p_ref, qn_ref, kn_ref, cos_ref, sin_ref, perm_ref, q_ref, k_ref, v_ref):
    cos = cos_ref[...]
    sin = sin_ref[...]
    perm2 = perm_ref[...]
    Dh = AT_HEAD_DIM
    for h in range(AT_HEADS):
        x = _rope_mxu(_rms(p_ref[0, :, h * Dh:(h + 1) * Dh], qn_ref[...]), cos, sin, perm2)
        q_ref[0, :, h * Dh:(h + 1) * Dh] = (x * (Dh ** -0.5)).astype(q_ref.dtype)
    for h in range(AT_KV_HEADS):
        off = (AT_HEADS + h) * Dh
        x = _rope_mxu(_rms(p_ref[0, :, off:off + Dh], kn_ref[...]), cos, sin, perm2)
        k_ref[0, :, h * Dh:(h + 1) * Dh] = x.astype(k_ref.dtype)
    for h in range(AT_KV_HEADS):
        off = (AT_HEADS + AT_KV_HEADS + h) * Dh
        v_ref[0, :, 2 * h * Dh:(2 * h + 1) * Dh] = p_ref[0, :, off:off + Dh].astype(v_ref.dtype)
        v_ref[0, :, (2 * h + 1) * Dh:(2 * h + 2) * Dh] = jnp.ones((p_ref.shape[1], Dh), v_ref.dtype)


def _qkprep(p, q_norm, k_norm, cos, sin):
    b, s, w = p.shape
    L = ROW_TILE
    Dh = AT_HEAD_DIM
    qd, kd = AT_HEADS * Dh, AT_KV_HEADS * Dh
    return pl.pallas_call(
        _qkprep_kernel,
        out_shape=(jax.ShapeDtypeStruct((b, s, qd), BF16),
                   jax.ShapeDtypeStruct((b, s, kd), BF16),
                   jax.ShapeDtypeStruct((b, s, 2 * kd), BF16)),
        grid=(b, s // L),
        in_specs=[pl.BlockSpec((1, L, w), lambda b, i: (b, i, 0)),
                  pl.BlockSpec((1, Dh), lambda b, i: (0, 0)),
                  pl.BlockSpec((1, Dh), lambda b, i: (0, 0)),
                  pl.BlockSpec((L, Dh), lambda b, i: (i, 0)),
                  pl.BlockSpec((L, Dh), lambda b, i: (i, 0)),
                  pl.BlockSpec((2 * Dh, Dh), lambda b, i: (0, 0))],
        out_specs=(pl.BlockSpec((1, L, qd), lambda b, i: (b, i, 0)),
                   pl.BlockSpec((1, L, kd), lambda b, i: (b, i, 0)),
                   pl.BlockSpec((1, L, 2 * kd), lambda b, i: (b, i, 0))),
        compiler_params=_params(("parallel", "parallel")),
        name="qkprep",
    )(p, q_norm.reshape(1, Dh), k_norm.reshape(1, Dh), cos, sin, _rotate_half_matrix())


def _rotate_half_matrix():
    half = ROPE_AXIS_DIM // 2
    i = jnp.arange(AT_HEAD_DIM)[:, None]
    j = jnp.arange(AT_HEAD_DIM)[None, :]
    first = (j % ROPE_AXIS_DIM) < half
    p = jnp.where(first & (i == j + half), -1.0, 0.0) + jnp.where(~first & (i == j - half), 1.0, 0.0)
    return jnp.concatenate([p, p], axis=0).astype(BF16)


def _attend(q_ref, k, v, o_ref):
    Dh = AT_HEAD_DIM
    for g in range(AT_GROUP):
        q = q_ref[0, :, g * Dh:(g + 1) * Dh]
        s = lax.dot_general(q, k, (((1,), (1,)), ((), ())), preferred_element_type=F32)
        e = jnp.exp((s - jnp.max(s, axis=-1, keepdims=True)).astype(BF16))
        o = jnp.dot(e, v, preferred_element_type=F32)
        o_ref[0, :, g * Dh:(g + 1) * Dh] = (o[:, :Dh] / o[:, Dh:]).astype(o_ref.dtype)


def _attn_kernel(q_ref, k_ref, v_ref, o_ref):
    i = pl.program_id(2)

    @pl.when(i == 0)
    def _():
        _attend(q_ref, k_ref[0, 0:ROW_TILE, :], v_ref[0, 0:ROW_TILE, :], o_ref)

    @pl.when(i > 0)
    def _():
        _attend(q_ref, k_ref[0], v_ref[0], o_ref)


def _attention(q, k, v):
    b, s, qd = q.shape
    L = ROW_TILE
    Dh = AT_HEAD_DIM
    gw = AT_GROUP * Dh
    return pl.pallas_call(
        _attn_kernel,
        out_shape=jax.ShapeDtypeStruct((b, s, qd), BF16),
        grid=(b, AT_KV_HEADS, s // L),
        in_specs=[pl.BlockSpec((1, L, gw), lambda b, h, i: (b, i, h)),
                  pl.BlockSpec((1, s, Dh), lambda b, h, i: (b, 0, h)),
                  pl.BlockSpec((1, s, 2 * Dh), lambda b, h, i: (b, 0, h))],
        out_specs=pl.BlockSpec((1, L, gw), lambda b, h, i: (b, i, h)),
        compiler_params=_params(("parallel", "parallel", "parallel")),
        name="attention",
    )(q, k, v)


def _cast_kernel(w_ref, o_ref):
    o_ref[...] = w_ref[...].astype(o_ref.dtype)


def _cast_bf16(w, layer):
    _, k, n = w.shape
    tk = math.gcd(k, 512)
    return pl.pallas_call(
        _cast_kernel,
        out_shape=jax.ShapeDtypeStruct((k, n), BF16),
        grid=(k // tk,),
        in_specs=[pl.BlockSpec((None, tk, n), lambda i: (layer, i, 0))],
        out_specs=pl.BlockSpec((tk, n), lambda i: (i, 0)),
        compiler_params=_params(("parallel",)),
        name="cast_bf16",
    )(w)


def _outproj_kernel(a_ref, w_ref, x_ref, g_ref, m_ref, wr_ref, xo_ref, h_ref, lg_ref):
    m = m_ref[0, 0]
    y = jnp.dot(a_ref[0], w_ref[...], preferred_element_type=F32)
    x = x_ref[0] + m[2:3] * y
    xo_ref[0] = x
    h = _rms(x, g_ref[...]) * (1.0 + m[4:5]) + m[3:4]
    hb = h.astype(h_ref.dtype)
    h_ref[0] = hb
    w2 = wr_ref[...]
    h_lo = (h - hb.astype(F32)).astype(BF16)
    both = jnp.dot(hb, w2, preferred_element_type=F32)
    lg_ref[0] = (both[:, :LANES] + both[:, LANES:]
                 + jnp.dot(h_lo, w2[:, :LANES], preferred_element_type=F32))


def _split_router(w_router):
    w = jnp.pad(w_router, ((0, 0), (0, 0), (0, LANES - w_router.shape[2])))
    hi = w.astype(BF16)
    lo = (w - hi.astype(F32)).astype(BF16)
    return jnp.concatenate([hi, lo], axis=2)


def _outproj(a, w, xs, g, mt, wr):
    b, s, d = xs.shape
    L = ROW_TILE
    tile = pl.BlockSpec((1, L, d), lambda b, i: (b, i, 0))
    return pl.pallas_call(
        _outproj_kernel,
        out_shape=(jax.ShapeDtypeStruct((b, s, d), F32),
                   jax.ShapeDtypeStruct((b, s, d), BF16),
                   jax.ShapeDtypeStruct((b, s, LANES), F32)),
        grid=(b, s // L),
        in_specs=[tile, pl.BlockSpec((d, d), lambda b, i: (0, 0)), tile,
                  pl.BlockSpec((1, d), lambda b, i: (0, 0)), _mod_spec(d),
                  pl.BlockSpec((d, 2 * LANES), lambda b, i: (0, 0))],
        out_specs=(tile, tile, pl.BlockSpec((1, L, LANES), lambda b, i: (b, i, 0))),
        compiler_params=_params(("parallel", "parallel")),
        name="outproj",
    )(a, w, xs, g.reshape(1, d), mt, wr)


def _excl_prefix(mask):
    e, n = mask.shape
    groups = n // LANES
    r = lax.broadcasted_iota(jnp.int32, (LANES, LANES), 0)
    c = lax.broadcasted_iota(jnp.int32, (LANES, LANES), 1)
    upper = jnp.where(r < c, 1.0, 0.0).astype(BF16)
    stk = jnp.concatenate([mask[:, g * LANES:(g + 1) * LANES] for g in range(groups)], axis=0)
    within = jnp.dot(stk.astype(BF16), upper, preferred_element_type=F32)
    tot = jnp.sum(stk, axis=1, keepdims=True)
    outs = []
    off = jnp.zeros((e, 1), F32)
    for g in range(groups):
        outs.append(within[g * e:(g + 1) * e] + off)
        off = off + tot[g * e:(g + 1) * e]
    return jnp.concatenate(outs, axis=1)


def _top_slots(aff, cap):
    v = lax.bitcast_convert_type(aff, jnp.int32)
    thr = jnp.zeros((aff.shape[0], 1), jnp.int32)
    for bit in range(30, -1, -1):
        cand = thr | (1 << bit)
        cnt = jnp.sum(jnp.where(v >= cand, 1.0, 0.0), axis=1, keepdims=True)
        thr = jnp.where(cnt >= cap, cand, thr)
    gt = jnp.where(v > thr, 1.0, 0.0)
    eq = jnp.where(v == thr, 1.0, 0.0)
    need = cap - jnp.sum(gt, axis=1, keepdims=True)
    sel = gt + eq * jnp.where(_excl_prefix(eq) < need, 1.0, 0.0)
    return jnp.where(sel > 0.0, _excl_prefix(sel), -1.0)


def _route_kernel(lg_ref, pos_ref, aff_ref, post_ref):
    s = lg_ref.shape[1]
    E = N_EXPERTS
    lt = lg_ref[0].T[0:E, :]
    ex = jnp.exp(lt - jnp.max(lt, axis=0, keepdims=True))
    aff = ex / jnp.sum(ex, axis=0, keepdims=True)
    aff_ref[0] = aff
    pos = jnp.concatenate(
        [_top_slots(aff[:, :ROW_TILE], EC_FACTOR * ROW_TILE // E),
         _top_slots(aff[:, ROW_TILE:], EC_FACTOR * (s - ROW_TILE) // E)], axis=1)
    pos_ref[0] = pos
    padded = jnp.concatenate([pos, jnp.full((LANES - E, s), -1.0, F32)], axis=0)
    post_ref[0] = padded.T


def _route(logits):
    b, s, _ = logits.shape
    E = N_EXPERTS
    return pl.pallas_call(
        _route_kernel,
        out_shape=(jax.ShapeDtypeStruct((b, E, s), F32),
                   jax.ShapeDtypeStruct((b, E, s), F32),
                   jax.ShapeDtypeStruct((b, s, LANES), F32)),
        grid=(b,),
        in_specs=[pl.BlockSpec((1, s, LANES), lambda b: (b, 0, 0))],
        out_specs=(pl.BlockSpec((1, E, s), lambda b: (b, 0, 0)),
                   pl.BlockSpec((1, E, s), lambda b: (b, 0, 0)),
                   pl.BlockSpec((1, s, LANES), lambda b: (b, 0, 0))),
        compiler_params=_params(("parallel",)),
        name="route",
    )(logits)


def _gather_kernel(h_ref, pos_ref, aff_ref, xx_ref, xc_ref, gx_ref, gc_ref):
    s = h_ref.shape[1]
    group = xx_ref.shape[0]
    for lo, n, x_ref, g_ref in ((0, ROW_TILE, xc_ref, gc_ref), (ROW_TILE, s - ROW_TILE, xx_ref, gx_ref)):
        cap = x_ref.shape[1]
        slot = lax.broadcasted_iota(jnp.int32, (cap, n), 0).astype(F32)
        hits = [pos_ref[0, j, :, lo:lo + n] == slot for j in range(group)]
        onehot = jnp.concatenate([jnp.where(hit, 1.0, 0.0).astype(BF16) for hit in hits], axis=0)
        rows = jnp.dot(onehot, h_ref[0, lo:lo + n, :], preferred_element_type=F32)
        for j in range(group):
            x_ref[j] = rows[j * cap:(j + 1) * cap].astype(x_ref.dtype)
            g_ref[j] = jnp.sum(jnp.where(hits[j], aff_ref[0, j, :, lo:lo + n], 0.0), axis=1, keepdims=True)


def _gather(h, pos, aff):
    b, s, d = h.shape
    E = N_EXPERTS
    G = GATHER_GROUP
    cap_c = EC_FACTOR * ROW_TILE // E
    cap_x = EC_FACTOR * (s - ROW_TILE) // E
    row = pl.BlockSpec((1, G, 1, s), lambda b, e: (b, e, 0, 0))
    return pl.pallas_call(
        _gather_kernel,
        out_shape=(jax.ShapeDtypeStruct((E, b * cap_x, d), BF16),
                   jax.ShapeDtypeStruct((E, b * cap_c, d), BF16),
                   jax.ShapeDtypeStruct((E, b * cap_x, 1), F32),
                   jax.ShapeDtypeStruct((E, b * cap_c, 1), F32)),
        grid=(b, E // G),
        in_specs=[pl.BlockSpec((1, s, d), lambda b, e: (b, 0, 0)), row, row],
        out_specs=(pl.BlockSpec((G, cap_x, d), lambda b, e: (e, b, 0)),
                   pl.BlockSpec((G, cap_c, d), lambda b, e: (e, b, 0)),
                   pl.BlockSpec((G, cap_x, 1), lambda b, e: (e, b, 0)),
                   pl.BlockSpec((G, cap_c, 1), lambda b, e: (e, b, 0))),
        compiler_params=_params(("parallel", "arbitrary")),
        name="gather",
    )(h, pos.reshape(b, E, 1, s), aff.reshape(b, E, 1, s))


def _ffn_kernel(n_sets, *refs):
    x_refs, g_refs = refs[:n_sets], refs[n_sets:2 * n_sets]
    wg_ref, wu_ref, wd_ref = refs[2 * n_sets:2 * n_sets + 3]
    y_refs, a_refs = refs[2 * n_sets + 3:3 * n_sets + 3], refs[3 * n_sets + 3:]
    step = pl.program_id(1)
    ax_ref = a_refs[0]
    nf = ax_ref.shape[0]
    pairs = tuple(zip(x_refs, a_refs, g_refs, y_refs))

    @pl.when(step < nf)
    def _():
        wg = wg_ref[0].astype(BF16)
        wu = wu_ref[0].astype(BF16)
        for x_ref, a_ref, _, _ in pairs:
            x = x_ref[0]
            g = jnp.dot(x, wg, preferred_element_type=F32)
            u = jnp.dot(x, wu, preferred_element_type=F32)
            a_ref[step] = (g * _sigmoid(g) * u).astype(BF16)

    @pl.when(step >= nf)
    def _():
        wd = wd_ref[0].astype(BF16)
        tf = ax_ref.shape[2]
        for _, a_ref, g_ref, y_ref in pairs:
            y = jnp.dot(a_ref[0], wd[0:tf], preferred_element_type=F32)
            for f in range(1, nf):
                y = y + jnp.dot(a_ref[f], wd[f * tf:(f + 1) * tf], preferred_element_type=F32)
            y_ref[0] = (y * g_ref[0]).astype(y_ref.dtype)


def _ffn(xs, gs, w_gate, w_up, w_down, layer):
    E, _, d = xs[0].shape
    rows = [x.shape[1] for x in xs]
    ff = w_gate.shape[3]
    nf = ff // FFN_HIDDEN_TILE
    nd = d // FFN_OUT_TILE
    xspec = lambda r, w: pl.BlockSpec((1, r, w), lambda e, s: (e, 0, 0))
    hid = lambda e, s: (layer, e, 0, jnp.minimum(s, nf - 1))
    out_tile = lambda s: jnp.maximum(s - nf, 0)
    yspec = lambda r: pl.BlockSpec((1, r, FFN_OUT_TILE), lambda e, s: (e, 0, out_tile(s)))
    return pl.pallas_call(
        functools.partial(_ffn_kernel, len(xs)),
        out_shape=tuple(jax.ShapeDtypeStruct((E, r, d), BF16) for r in rows),
        grid=(E, nf + nd),
        in_specs=[xspec(r, d) for r in rows] + [xspec(r, 1) for r in rows] + [
            pl.BlockSpec((None, 1, d, FFN_HIDDEN_TILE), hid),
            pl.BlockSpec((None, 1, d, FFN_HIDDEN_TILE), hid),
            pl.BlockSpec((None, 1, ff, FFN_OUT_TILE), lambda e, s: (layer, e, 0, out_tile(s)))],
        out_specs=tuple(yspec(r) for r in rows),
        scratch_shapes=[pltpu.VMEM((nf, r, FFN_HIDDEN_TILE), BF16) for r in rows],
        compiler_params=_params(("parallel", "arbitrary")),
        name="expert_ffn",
    )(*xs, *gs, w_gate, w_up, w_down)


def _combine(pt, y_ref):
    cap = y_ref.shape[1]
    n = pt.shape[0]
    slot = lax.broadcasted_iota(jnp.int32, (n, cap), 1).astype(F32)
    acc = jnp.zeros((n, y_ref.shape[2]), F32)
    for e in range(N_EXPERTS):
        onehot = jnp.where(pt[:, e:e + 1] == slot, 1.0, 0.0).astype(BF16)
        acc = acc + jnp.dot(onehot, y_ref[e], preferred_element_type=F32)
    return acc


def _scatter_kernel(x_ref, pt_ref, yx_ref, yc_ref, m_ref, gn_ref, mn_ref, o_ref, h_ref):
    i = pl.program_id(1)
    gate = m_ref[0, 0][5:6]
    mn = mn_ref[0, 0]

    def finish(acc):
        x = x_ref[0] + gate * acc
        o_ref[0] = x
        h_ref[0] = (_rms(x, gn_ref[...]) * (1.0 + mn[1:2]) + mn[0:1]).astype(h_ref.dtype)

    @pl.when(i == 0)
    def _():
        finish(_combine(pt_ref[0], yc_ref))

    @pl.when(i > 0)
    def _():
        finish(_combine(pt_ref[0], yx_ref))


def _scatter(xs, post, yx, yc, mt, g_next, mt_next):
    b, s, d = xs.shape
    E = N_EXPERTS
    L = ROW_TILE
    cap_x = yx.shape[1] // b
    cap_c = yc.shape[1] // b
    tile = pl.BlockSpec((1, L, d), lambda b, i: (b, i, 0))
    return pl.pallas_call(
        _scatter_kernel,
        out_shape=(jax.ShapeDtypeStruct((b, s, d), F32), jax.ShapeDtypeStruct((b, s, d), BF16)),
        grid=(b, s // L),
        in_specs=[tile,
                  pl.BlockSpec((1, L, LANES), lambda b, i: (b, i, 0)),
                  pl.BlockSpec((E, cap_x, d), lambda b, i: (0, b, 0)),
                  pl.BlockSpec((E, cap_c, d), lambda b, i: (0, b, 0)),
                  _mod_spec(d),
                  pl.BlockSpec((1, d), lambda b, i: (0, 0)),
                  _mod_spec(d)],
        out_specs=(tile, tile),
        compiler_params=_params(("parallel", "arbitrary")),
        name="scatter",
    )(xs, post, yx, yc, mt, g_next.reshape(1, d), mt_next)


def _scatter_last_kernel(x_ref, pt_ref, yx_ref, m_ref, o_ref):
    o_ref[0] = x_ref[0] + m_ref[0, 0][5:6] * _combine(pt_ref[0], yx_ref)


def _scatter_last(xs, post, yx, mt):
    b, s, d = xs.shape
    E = N_EXPERTS
    L = ROW_TILE
    cap_x = yx.shape[1] // b
    return pl.pallas_call(
        _scatter_last_kernel,
        out_shape=jax.ShapeDtypeStruct((b, s - L, d), F32),
        grid=(b, s // L - 1),
        in_specs=[pl.BlockSpec((1, L, d), lambda b, i: (b, i + 1, 0)),
                  pl.BlockSpec((1, L, LANES), lambda b, i: (b, i + 1, 0)),
                  pl.BlockSpec((E, cap_x, d), lambda b, i: (0, b, 0)),
                  pl.BlockSpec((1, 1, 6, d), lambda b, i: (b, 1, 0, 0))],
        out_specs=pl.BlockSpec((1, L, d), lambda b, i: (b, i, 0)),
        compiler_params=_params(("parallel", "arbitrary")),
        name="scatter_last",
    )(xs, post, yx, mt)


def kernel(x, c, ctx, c_ctx, w_mod, b_mod, norm_mix, norm_ffn, ab_w_in, ab_gate_b, ml_norm,
           rg_conv_w, rg_conv_b, rg_wa, rg_ba, rg_wx, rg_bx, rg_lam, ab_w_out,
           at_w_qkv, at_q_norm, at_k_norm, at_w_o,
           moe_w_router, moe_w_gate, moe_w_up, moe_w_down):
    B, T, D = x.shape
    n_ctx = ctx.shape[1]
    assert n_ctx == ROW_TILE and T % ROW_TILE == 0 and B <= 7
    S = n_ctx + T
    depth = w_mod.shape[0]
    E = moe_w_router.shape[2]

    c8 = jnp.concatenate([c, c_ctx[None], jnp.zeros((7 - B, D), F32)], axis=0)
    modv = _mods(c8, w_mod, b_mod).reshape(depth, 8, 6, D)
    mod_c = jnp.broadcast_to(modv[:, B][:, None], (depth, B, 6, D))
    mtab = jnp.stack([mod_c, modv[:, :B]], axis=2)
    cos, sin = _rope_tables(n_ctx, T)
    wr2 = _split_router(moe_w_router)
    w_in_t = jnp.swapaxes(ab_w_in, 1, 2)

    xs, h = _norm_mod(ctx, x, norm_mix[0], mtab[0])
    for l in range(depth):
        j = l // 2
        mt = mtab[l]
        h = h.reshape(B * S, D)
        if l % 2 == 0:
            g0 = 4 * ML_WIDTH
            p1 = _matmul_t(h, w_in_t, j, 0, g0, tn=512, out_dtype=BF16, name="ab_in_qkvo").reshape(B, S, g0)
            p2 = _matmul_t(h, w_in_t, j, g0 + N_GATES, 2 * RG_WIDTH, tn=512,
                           name="ab_in_rg").reshape(B, S, 2 * RG_WIDTH)
            gates = _matmul_t(h, w_in_t, j, g0, N_GATES, tn=N_GATES, name="ab_in_gates").reshape(B, S, N_GATES)
            gcol = gates + ab_gate_b[j].reshape(-1)
            hml = _mlstm(p1, gcol, jnp.swapaxes(gcol, 1, 2))
            hrg = _rglru(p2, rg_conv_w[j], rg_conv_b[j], rg_wa[j], rg_ba[j], rg_wx[j], rg_bx[j], rg_lam[j])
            mixed = _merge(hml, hrg, p1, p2, ml_norm[j])
            w_out = _cast_bf16(ab_w_out, j)
        else:
            wq = at_w_qkv.shape[2]
            p = _matmul(h, at_w_qkv, wq, tn=512, layer=j, name="at_qkv").reshape(B, S, wq)
            q, k, v = _qkprep(p, at_q_norm[j], at_k_norm[j], cos, sin)
            mixed = _attention(q, k, v)
            w_out = _cast_bf16(at_w_o, j)
        xs, h2, logits = _outproj(mixed, w_out, xs, norm_ffn[l], mt, wr2[l])
        pos, aff, post = _route(logits)
        xx, xc, gx, gc = _gather(h2, pos, aff)
        if l + 1 == depth:
            yx, = _ffn((xx,), (gx,), moe_w_gate, moe_w_up, moe_w_down, l)
            return _scatter_last(xs, post, yx, mt)
        yx, yc = _ffn((xx, xc), (gx, gc), moe_w_gate, moe_w_up, moe_w_down, l)
        xs, h = _scatter(xs, post, yx, yc, mt, norm_mix[l + 1], mtab[l + 1])
```
